```python
import jax
import jax.numpy as jnp
from jax import lax
import numpy as np

D_MODEL = 4096
BATCH = 8
SEQ = 2048
DEPTH = 2

HEAD_DIM = 128
N_HEADS_TOTAL = D_MODEL // HEAD_DIM
N_HEADS_ATTN = N_HEADS_TOTAL // 2
N_HEADS_GMLP = N_HEADS_TOTAL - N_HEADS_ATTN
D_ATTN = N_HEADS_ATTN * HEAD_DIM
D_GMLP = N_HEADS_GMLP * HEAD_DIM
D_MIX = D_ATTN + D_GMLP
D_IN_PROJ = 3 * D_ATTN + 2 * D_GMLP
ROT_DIM = HEAD_DIM // 4
ROPE_THETA = 500000.0
DILATED_PATTERNS = ((128, 1), (512, 4), (2048, 16))
ATTN_BLOCK = 128
GMLP_CHUNK = 128
N_EXPERTS = 32
TOP_K = 4
D_FF_EXPERT = 768
SWIGLU_LIMIT = 7.0
SWIGLU_ALPHA = 1.702
MOE_BLOCK = 128
N_MOD = 6
EPS = 1e-6

kernel_name = 'hybrid_dilated_attn_gmlp_moe_adaln'


def rms_norm(x, gain=None):
    xf = x.astype(jnp.float32)
    y = xf * lax.rsqrt(jnp.mean(xf * xf, axis=-1, keepdims=True) + EPS)
    if gain is not None:
        y = y * gain.astype(jnp.float32)
    return y.astype(x.dtype)


def layer_norm(x, gain, bias):
    xf = x.astype(jnp.float32)
    xc = xf - jnp.mean(xf, axis=-1, keepdims=True)
    y = xc * lax.rsqrt(jnp.mean(xc * xc, axis=-1, keepdims=True) + EPS)
    return (y * gain.astype(jnp.float32) + bias.astype(jnp.float32)).astype(x.dtype)


def partial_rope(t, positions):
    half = ROT_DIM // 2
    inv_freq = ROPE_THETA ** (-jnp.arange(half, dtype=jnp.float32) * 2.0 / ROT_DIM)
    ang = positions.astype(jnp.float32)[..., None] * inv_freq
    cos = jnp.cos(ang)[:, :, None, :].astype(t.dtype)
    sin = jnp.sin(ang)[:, :, None, :].astype(t.dtype)
    t1, t2, rest = t[..., :half], t[..., half:ROT_DIM], t[..., ROT_DIM:]
    return jnp.concatenate([t1 * cos - t2 * sin, t2 * cos + t1 * sin, rest], axis=-1)


def _dilated_branch(q, k, v, dilation, steps):
    B, S, H, E = q.shape
    L = S // dilation
    nb = -(-L // ATTN_BLOCK)
    Lp = nb * ATTN_BLOCK

    def to_blocks(t):
        t = t.reshape(B, L, dilation, H, E).transpose(0, 2, 1, 3, 4)
        t = jnp.pad(t, ((0, 0), (0, 0), (0, Lp - L), (0, 0), (0, 0)))
        return t.reshape(B, dilation, nb, ATTN_BLOCK, H, E)

    def with_prev(t):
        prev = jnp.pad(t, ((0, 0), (0, 0), (1, 0), (0, 0), (0, 0), (0, 0)))[:, :, :-1]
        return jnp.concatenate([prev, t], axis=3)

    qb = to_blocks(q)
    kc = with_prev(to_blocks(k))
    vc = with_prev(to_blocks(v))
    s = jnp.einsum('brnqhe,brnkhe->brnhqk', qb, kc, preferred_element_type=jnp.float32)
    qi = jnp.arange(ATTN_BLOCK)[:, None]
    kj = jnp.arange(2 * ATTN_BLOCK)[None, :]
    dist = qi + ATTN_BLOCK - kj
    blk = jnp.arange(nb)[:, None, None]
    mask = (dist >= 0) & (dist <= steps) & (blk * ATTN_BLOCK + kj >= ATTN_BLOCK)
    s = jnp.where(mask[:, None], s, -jnp.inf)
    m = jnp.max(s, axis=-1, keepdims=True)
    p = jnp.exp(s - m)
    l = jnp.sum(p, axis=-1, keepdims=True)
    o = jnp.einsum('brnhqk,brnkhe->brnqhe', p, vc, preferred_element_type=jnp.float32)
    o = o / jnp.swapaxes(l, 3, 4)
    lse = jnp.swapaxes((m + jnp.log(l))[..., 0], 3, 4)

    def from_blocks(t):
        t = t.reshape((B, dilation, Lp) + t.shape[4:])[:, :, :L]
        return jnp.moveaxis(t, 1, 2).reshape((B, S) + t.shape[3:])

    return from_blocks(o), from_blocks(lse)


def dilated_attention(q, k, v):
    outs, lses = [], []
    for window, dilation in DILATED_PATTERNS:
        o, lse = _dilated_branch(q, k, v, dilation, window // dilation)
        outs.append(o)
        lses.append(lse)
    w = jax.nn.softmax(jnp.stack(lses, axis=0), axis=0)
    out = jnp.sum(w[..., None] * jnp.stack(outs, axis=0), axis=0)
    return out.astype(q.dtype)


def spatial_gating(u, v, w_s, b_s, ln_g, ln_b):
    B, S, G, E = u.shape
    v = layer_norm(v, ln_g, ln_b)
    vc = v.reshape(B, S // GMLP_CHUNK, GMLP_CHUNK, G, E)
    ws = jnp.tril(w_s)
    mixed = jnp.einsum('gqk,bckge->bcqge', ws, vc) + jnp.transpose(b_s)[None, None, :, :, None]
    return u * mixed.reshape(B, S, G, E)


def moe_ffn(h, w_router, b_router, w_gate_up, b_gate_up, w_down, b_down):
    B, S, D = h.shape
    N = B * S
    xf = h.reshape(N, D)
    logits = (xf @ w_router + b_router).astype(jnp.float32)
    top_logit, top_idx = lax.top_k(logits, TOP_K)
    gates = jax.nn.softmax(top_logit, axis=-1).astype(h.dtype)
    n_assign = N * TOP_K
    flat_e = top_idx.reshape(-1)
    flat_tok = jnp.repeat(jnp.arange(N, dtype=jnp.int32), TOP_K)
    order = jnp.argsort(flat_e)
    sorted_e = flat_e[order]
    counts = jnp.bincount(flat_e, length=N_EXPERTS)
    padded = (counts + MOE_BLOCK - 1) // MOE_BLOCK * MOE_BLOCK
    start = jnp.cumsum(counts) - counts
    padded_end = jnp.cumsum(padded)
    padded_start = padded_end - padded
    dest = padded_start[sorted_e] + jnp.arange(n_assign) - start[sorted_e]
    n_slots = n_assign + N_EXPERTS * MOE_BLOCK
    n_blocks = n_slots // MOE_BLOCK
    slot_tok = jnp.full((n_slots,), N, jnp.int32).at[dest].set(flat_tok[order])
    slot_gate = jnp.zeros((n_slots,), h.dtype).at[dest].set(gates.reshape(-1)[order])
    block_e = jnp.minimum(
        jnp.searchsorted(padded_end, jnp.arange(n_blocks) * MOE_BLOCK, side='right'), N_EXPERTS - 1)
    x_pad = jnp.concatenate([xf, jnp.zeros((1, D), xf.dtype)], axis=0)

    def expert_block(args):
        tok, g, e = args
        xb = x_pad[tok]
        gu = xb @ w_gate_up[e] + b_gate_up[e]
        gate = jnp.minimum(gu[:, 0::2], SWIGLU_LIMIT)
        up = jnp.clip(gu[:, 1::2], -SWIGLU_LIMIT, SWIGLU_LIMIT)
        act = (up + 1) * gate * jax.nn.sigmoid(SWIGLU_ALPHA * gate)
        return (act @ w_down[e] + b_down[e]) * g[:, None]

    y = lax.map(expert_block, (slot_tok.reshape(n_blocks, MOE_BLOCK),
                               slot_gate.reshape(n_blocks, MOE_BLOCK), block_e))
    out = jax.ops.segment_sum(y.reshape(n_slots, D), slot_tok, num_segments=N + 1)[:N]
    return out.reshape(B, S, D)


def setup_inputs(seed: int = 0) -> dict:
    key = jax.random.key(seed)
    ks = jax.random.split(key, 21)

    def nrm(k, shape, scale):
        return jax.random.normal(k, shape, jnp.float32) * scale

    x = nrm(ks[0], (BATCH, SEQ, D_MODEL), 1.0)
    c = nrm(ks[1], (BATCH, D_MODEL), 1.0)
    positions = (jax.random.randint(ks[2], (BATCH, 1), 0, 4096, dtype=jnp.int32)
                 + jnp.arange(SEQ, dtype=jnp.int32)[None, :])
    w_mod = nrm(ks[3], (D_MODEL, N_MOD * D_MODEL), 0.5 * D_MODEL ** -0.5)
    b_mod = nrm(ks[4], (N_MOD * D_MODEL,), 0.02)
    mod_layer = nrm(ks[5], (DEPTH, N_MOD, D_MODEL), 0.1)
    w_in = nrm(ks[6], (DEPTH, D_MODEL, D_IN_PROJ), D_MODEL ** -0.5)
    q_norm = 1.0 + nrm(ks[7], (DEPTH, HEAD_DIM), 0.02)
    k_norm = 1.0 + nrm(ks[8], (DEPTH, HEAD_DIM), 0.02)
    w_s = nrm(ks[9], (DEPTH, N_HEADS_GMLP, GMLP_CHUNK, GMLP_CHUNK), GMLP_CHUNK ** -0.5)
    b_s = 1.0 + nrm(ks[10], (DEPTH, N_HEADS_GMLP, GMLP_CHUNK), 0.02)
    v_ln_g = 1.0 + nrm(ks[11], (DEPTH, N_HEADS_GMLP, HEAD_DIM), 0.02)
    v_ln_b = nrm(ks[12], (DEPTH, N_HEADS_GMLP, HEAD_DIM), 0.02)
    out_norm = 1.0 + nrm(ks[13], (DEPTH, D_MIX), 0.02)
    w_out = nrm(ks[14], (DEPTH, D_MIX, D_MODEL), D_MIX ** -0.5)
    w_router = nrm(ks[15], (DEPTH, D_MODEL, N_EXPERTS), D_MODEL ** -0.5)
    b_router = nrm(ks[16], (DEPTH, N_EXPERTS), 0.01)
    w_gate_up = nrm(ks[17], (DEPTH, N_EXPERTS, D_MODEL, 2 * D_FF_EXPERT), D_MODEL ** -0.5)
    b_gate_up = nrm(ks[18], (DEPTH, N_EXPERTS, 2 * D_FF_EXPERT), 0.01)
    w_down = nrm(ks[19], (DEPTH, N_EXPERTS, D_FF_EXPERT, D_MODEL), D_FF_EXPERT ** -0.5)
    b_down = nrm(ks[20], (DEPTH, N_EXPERTS, D_MODEL), 0.01)
    return {'x': x, 'c': c, 'positions': positions, 'w_mod': w_mod, 'b_mod': b_mod,
            'mod_layer': mod_layer, 'w_in': w_in, 'q_norm': q_norm, 'k_norm': k_norm,
            'w_s': w_s, 'b_s': b_s, 'v_ln_g': v_ln_g, 'v_ln_b': v_ln_b, 'out_norm': out_norm,
            'w_out': w_out, 'w_router': w_router, 'b_router': b_router,
            'w_gate_up': w_gate_up, 'b_gate_up': b_gate_up, 'w_down': w_down, 'b_down': b_down}


def reference(x, c, positions, w_mod, b_mod, mod_layer, w_in, q_norm, k_norm, w_s, b_s,
              v_ln_g, v_ln_b, out_norm, w_out, w_router, b_router, w_gate_up, b_gate_up,
              w_down, b_down):
    B, S, D = x.shape
    mod_base = (jax.nn.silu(c) @ w_mod + b_mod).reshape(B, N_MOD, D)
    splits = [D_ATTN, 2 * D_ATTN, 3 * D_ATTN, 3 * D_ATTN + D_GMLP]
    for l in range(DEPTH):
        mod = mod_base + mod_layer[l][None]
        shift_m, scale_m, gate_m, shift_f, scale_f, gate_f = [mod[:, i, None, :] for i in range(N_MOD)]

        h = rms_norm(x) * (1 + scale_m) + shift_m
        proj = h @ w_in[l]
        q, k, v, gu, gv = jnp.split(proj, splits, axis=-1)
        q = q.reshape(B, S, N_HEADS_ATTN, HEAD_DIM)
        k = k.reshape(B, S, N_HEADS_ATTN, HEAD_DIM)
        v = v.reshape(B, S, N_HEADS_ATTN, HEAD_DIM)
        q = partial_rope(rms_norm(q, q_norm[l]), positions) * (HEAD_DIM ** -0.5)
        k = partial_rope(rms_norm(k, k_norm[l]), positions)
        a = dilated_attention(q, k, v)
        u = jax.nn.gelu(gu.reshape(B, S, N_HEADS_GMLP, HEAD_DIM), approximate=False)
        g = jax.nn.gelu(gv.reshape(B, S, N_HEADS_GMLP, HEAD_DIM), approximate=False)
        g = spatial_gating(u, g, w_s[l], b_s[l], v_ln_g[l], v_ln_b[l])
        a = rms_norm(a.reshape(B, S, D_ATTN), out_norm[l, :D_ATTN])
        g = rms_norm(g.reshape(B, S, D_GMLP), out_norm[l, D_ATTN:])
        x = x + gate_m * (jnp.concatenate([a, g], axis=-1) @ w_out[l])

        h = rms_norm(x) * (1 + scale_f) + shift_f
        x = x + gate_f * moe_ffn(h, w_router[l], b_router[l], w_gate_up[l], b_gate_up[l],
                                 w_down[l], b_down[l])
    return x
```

```python
import functools
import math

import jax
import jax.numpy as jnp
from jax import lax
from jax.experimental import pallas as pl
from jax.experimental.pallas import tpu as pltpu

F32 = jnp.float32
BF16 = jnp.bfloat16
I32 = jnp.int32

HEAD_DIM = 128
N_HEADS_ATTN = 16
N_GROUPS_GMLP = 16
D_ATTN = N_HEADS_ATTN * HEAD_DIM
D_GMLP = N_GROUPS_GMLP * HEAD_DIM
ROT_DIM = HEAD_DIM // 4
ROPE_THETA = 500000.0
DILATED_PATTERNS = ((128, 1), (512, 4), (2048, 16))
GMLP_CHUNK = 128
N_EXPERTS = 32
TOP_K = 4
SWIGLU_LIMIT = 7.0
SWIGLU_ALPHA = 1.702
N_MOD = 6
EPS = 1e-6
LANES = 128
MASKED = -1e30

VMEM_LIMIT = 56 * 1024 * 1024


def _params(sem, vmem=VMEM_LIMIT):
    return pltpu.CompilerParams(dimension_semantics=sem, vmem_limit_bytes=vmem)


def _mod_kernel(c_ref, w_ref, b_ref, o_ref):
    c = c_ref[...]
    s = c * jax.nn.sigmoid(c)
    o_ref[...] = jnp.dot(s.astype(BF16), w_ref[...].astype(BF16),
                         preferred_element_type=F32) + b_ref[...]


def _mod_base(c, w_mod, b_mod, tn=512):
    B, D = c.shape
    n_out = w_mod.shape[1]
    return pl.pallas_call(
        _mod_kernel,
        grid=(n_out // tn,),
        in_specs=[pl.BlockSpec((B, D), lambda j: (0, 0)),
                  pl.BlockSpec((D, tn), lambda j: (0, j)),
                  pl.BlockSpec((1, tn), lambda j: (0, j))],
        out_specs=pl.BlockSpec((B, tn), lambda j: (0, j)),
        out_shape=jax.ShapeDtypeStruct((B, n_out), F32),
        compiler_params=_params(("parallel",)),
        name="mod_base",
    )(c, w_mod, b_mod.reshape(1, n_out))


def _mod_row(mb_ref, ml_ref, idx):
    return mb_ref[0, idx:idx + 1, :] + ml_ref[idx:idx + 1, :]


def _norm_mod(x, mb_ref, ml_ref, shift_idx, scale_idx):
    y = x * lax.rsqrt(jnp.mean(x * x, axis=-1, keepdims=True) + EPS)
    return y * (1.0 + _mod_row(mb_ref, ml_ref, scale_idx)) + _mod_row(mb_ref, ml_ref, shift_idx)


def _norm_kernel(x_ref, mb_ref, ml_ref, h_ref):
    h_ref[...] = _norm_mod(x_ref[...], mb_ref, ml_ref, 0, 1).astype(h_ref.dtype)


def _mixer_norm(x2, mod_base, mod_l, S, tm=256):
    N, D = x2.shape
    per_b = S // tm
    return pl.pallas_call(
        _norm_kernel,
        grid=(N // tm,),
        in_specs=[pl.BlockSpec((tm, D), lambda i: (i, 0)),
                  pl.BlockSpec((1, N_MOD, D), lambda i: (i // per_b, 0, 0)),
                  pl.BlockSpec((N_MOD, D), lambda i: (0, 0))],
        out_specs=pl.BlockSpec((tm, D), lambda i: (i, 0)),
        out_shape=jax.ShapeDtypeStruct((N, D), BF16),
        compiler_params=_params(("parallel",)),
        name="mixer_norm",
    )(x2, mod_base, mod_l)


def _gelu(x):
    return 0.5 * x * (1.0 + lax.erf(x * (1.0 / math.sqrt(2.0))))


def _inproj_kernel(h_ref, w_ref, qn_ref, kn_ref, cos_ref, s1_ref, s2_ref, lg_ref, lb_ref,
                   o_ref, acc_ref, *, heads_per_tile, tiles_per_part):
    j = pl.program_id(1)
    part = j // tiles_per_part
    acc_ref[...] = jnp.dot(h_ref[...], w_ref[...], preferred_element_type=F32)

    def per_head(fn):
        for hh in range(heads_per_tile):
            sl = slice(hh * HEAD_DIM, (hh + 1) * HEAD_DIM)
            o_ref[:, sl] = fn(acc_ref[:, sl], sl).astype(o_ref.dtype)

    def qk_norm_rope(t, gain, out_scale):
        y = t * lax.rsqrt(jnp.mean(t * t, axis=-1, keepdims=True) + EPS) * gain
        r = (y * cos_ref[...] + pltpu.roll(y, HEAD_DIM - ROT_DIM // 2, 1) * s1_ref[...]
             + pltpu.roll(y, ROT_DIM // 2, 1) * s2_ref[...])
        return r * out_scale

    def gate_ln(t, sl):
        g = _gelu(t)
        gc = g - jnp.mean(g, axis=-1, keepdims=True)
        y = gc * lax.rsqrt(jnp.mean(gc * gc, axis=-1, keepdims=True) + EPS)
        return y * lg_ref[:, sl] + lb_ref[:, sl]

    @pl.when(part == 0)
    def _():
        per_head(lambda t, sl: qk_norm_rope(t, qn_ref[...], HEAD_DIM ** -0.5))

    @pl.when(part == 1)
    def _():
        per_head(lambda t, sl: qk_norm_rope(t, kn_ref[...], 1.0))

    @pl.when(part == 2)
    def _():
        per_head(lambda t, sl: t)

    @pl.when(part == 3)
    def _():
        per_head(lambda t, sl: _gelu(t))

    @pl.when(part == 4)
    def _():
        per_head(gate_ln)


def _in_proj(h, w_in, qn, kn, cos_t, s1_t, s2_t, ln_g, ln_b, bm=1024, bn=512):
    N, D = h.shape
    n_out = w_in.shape[1]
    tiles_per_part = D_ATTN // bn
    kern = functools.partial(_inproj_kernel, heads_per_tile=bn // HEAD_DIM,
                             tiles_per_part=tiles_per_part)
    ln_tile = lambda i, j: (0, jnp.maximum(j - 4 * tiles_per_part, 0))
    return pl.pallas_call(
        kern,
        grid=(N // bm, n_out // bn),
        in_specs=[pl.BlockSpec((bm, D), lambda i, j: (i, 0)),
                  pl.BlockSpec((D, bn), lambda i, j: (0, j)),
                  pl.BlockSpec((1, HEAD_DIM), lambda i, j: (0, 0)),
                  pl.BlockSpec((1, HEAD_DIM), lambda i, j: (0, 0)),
                  pl.BlockSpec((bm, HEAD_DIM), lambda i, j: (i, 0)),
                  pl.BlockSpec((bm, HEAD_DIM), lambda i, j: (i, 0)),
                  pl.BlockSpec((bm, HEAD_DIM), lambda i, j: (i, 0)),
                  pl.BlockSpec((1, bn), ln_tile),
                  pl.BlockSpec((1, bn), ln_tile)],
        out_specs=pl.BlockSpec((bm, bn), lambda i, j: (i, j)),
        out_shape=jax.ShapeDtypeStruct((N, n_out), BF16),
        scratch_shapes=[pltpu.VMEM((bm, bn), F32)],
        compiler_params=_params(("parallel", "arbitrary")),
        name="in_proj",
    )(h, w_in, qn, kn, cos_t, s1_t, s2_t, ln_g, ln_b)


def _attn_kernel(q_ref, k_ref, v_ref, bias_ref, o_ref, *, tq):
    S = q_ref.shape[1]
    n_q = S // tq
    for i in range(n_q):
        kend = (i + 1) * tq
        q = q_ref[0, i * tq:(i + 1) * tq, :]
        s = lax.dot_general(q, k_ref[0, :kend, :], (((1,), (1,)), ((), ())),
                            preferred_element_type=F32)
        off = (n_q - 1 - i) * tq
        s = s + bias_ref[:, off:off + kend]
        m = jnp.max(s, axis=-1, keepdims=True)
        p = jnp.exp(s - m)
        l = jnp.sum(p, axis=-1, keepdims=True)
        o = jnp.dot(p.astype(BF16), v_ref[0, :kend, :], preferred_element_type=F32)
        o_ref[0, i * tq:(i + 1) * tq, :] = (o / l).astype(o_ref.dtype)


def _attn_bias_table(S, tq):
    r = jnp.arange(tq, dtype=I32)[:, None]
    u = jnp.arange(S, dtype=I32)[None, :]
    d = r + (S - tq) - u
    mult = jnp.zeros((tq, S), F32)
    for window, dil in DILATED_PATTERNS:
        mult = mult + ((d >= 0) & (d % dil == 0) & (d <= window)).astype(F32)
    return jnp.where(mult > 0, jnp.log(jnp.maximum(mult, 1.0)), MASKED)


def _attention(proj3, tq=256):
    B, S, _ = proj3.shape
    bias = _attn_bias_table(S, tq)
    H = N_HEADS_ATTN
    return pl.pallas_call(
        functools.partial(_attn_kernel, tq=tq),
        grid=(B, H),
        in_specs=[pl.BlockSpec((1, S, HEAD_DIM), lambda b, h: (b, 0, h)),
                  pl.BlockSpec((1, S, HEAD_DIM), lambda b, h: (b, 0, H + h)),
                  pl.BlockSpec((1, S, HEAD_DIM), lambda b, h: (b, 0, 2 * H + h)),
                  pl.BlockSpec((tq, S), lambda b, h: (0, 0))],
        out_specs=pl.BlockSpec((1, S, HEAD_DIM), lambda b, h: (b, 0, h)),
        out_shape=jax.ShapeDtypeStruct((B, S, D_ATTN), BF16),
        compiler_params=_params(("parallel", "parallel")),
        name="dilated_attn",
    )(proj3, proj3, proj3, bias)


def _gmlp_kernel(a_ref, u_ref, gv_ref, ws_ref, bst_ref, on_ref, o_ref, g_scr):
    tg = a_ref.shape[0]
    row = lax.broadcasted_iota(I32, (GMLP_CHUNK, GMLP_CHUNK), 0)
    col = lax.broadcasted_iota(I32, (GMLP_CHUNK, GMLP_CHUNK), 1)
    causal = row >= col
    for g in range(N_GROUPS_GMLP):
        w = jnp.where(causal, ws_ref[g], 0.0).astype(BF16)
        bias = bst_ref[:, g:g + 1]
        sl = slice(g * HEAD_DIM, (g + 1) * HEAD_DIM)
        for c in range(tg // GMLP_CHUNK):
            rows = slice(c * GMLP_CHUNK, (c + 1) * GMLP_CHUNK)
            mixed = jnp.dot(w, gv_ref[rows, sl], preferred_element_type=F32) + bias
            g_scr[rows, sl] = u_ref[rows, sl].astype(F32) * mixed
    gm = g_scr[...]
    gm = gm * lax.rsqrt(jnp.mean(gm * gm, axis=-1, keepdims=True) + EPS)
    o_ref[:, D_ATTN:] = (gm * on_ref[:, D_ATTN:]).astype(o_ref.dtype)
    a = a_ref[...].astype(F32)
    a = a * lax.rsqrt(jnp.mean(a * a, axis=-1, keepdims=True) + EPS)
    o_ref[:, :D_ATTN] = (a * on_ref[:, :D_ATTN]).astype(o_ref.dtype)


def _gmlp_and_norms(attn2, proj, w_s, b_s_t, out_norm, tg=512):
    N = attn2.shape[0]
    D_mix = D_ATTN + D_GMLP
    u_blk = 3 * D_ATTN // D_GMLP
    return pl.pallas_call(
        _gmlp_kernel,
        grid=(N // tg,),
        in_specs=[pl.BlockSpec((tg, D_ATTN), lambda i: (i, 0)),
                  pl.BlockSpec((tg, D_GMLP), lambda i: (i, u_blk)),
                  pl.BlockSpec((tg, D_GMLP), lambda i: (i, u_blk + 1)),
                  pl.BlockSpec((N_GROUPS_GMLP, GMLP_CHUNK, GMLP_CHUNK), lambda i: (0, 0, 0)),
                  pl.BlockSpec((GMLP_CHUNK, N_GROUPS_GMLP), lambda i: (0, 0)),
                  pl.BlockSpec((1, D_mix), lambda i: (0, 0))],
        out_specs=pl.BlockSpec((tg, D_mix), lambda i: (i, 0)),
        out_shape=jax.ShapeDtypeStruct((N, D_mix), BF16),
        scratch_shapes=[pltpu.VMEM((tg, D_GMLP), F32)],
        compiler_params=_params(("parallel",)),
        name="gmlp_norms",
    )(attn2, proj, proj, w_s, b_s_t, out_norm)


def _outproj_kernel(m_ref, w_ref, x_ref, mb_ref, ml_ref, o_ref):
    acc = jnp.dot(m_ref[...], w_ref[...], preferred_element_type=F32)
    o_ref[...] = x_ref[...] + _mod_row(mb_ref, ml_ref, 2) * acc


def _out_proj(mix, w_out, x2, mod_base, mod_l, S, bm=1024, bn=512):
    N, D_mix = mix.shape
    D = w_out.shape[1]
    per_b = S // bm
    return pl.pallas_call(
        _outproj_kernel,
        grid=(N // bm, D // bn),
        in_specs=[pl.BlockSpec((bm, D_mix), lambda i, j: (i, 0)),
                  pl.BlockSpec((D_mix, bn), lambda i, j: (0, j)),
                  pl.BlockSpec((bm, bn), lambda i, j: (i, j)),
                  pl.BlockSpec((1, N_MOD, bn), lambda i, j: (i // per_b, 0, j)),
                  pl.BlockSpec((N_MOD, bn), lambda i, j: (0, j))],
        out_specs=pl.BlockSpec((bm, bn), lambda i, j: (i, j)),
        out_shape=jax.ShapeDtypeStruct((N, D), F32),
        compiler_params=_params(("parallel", "arbitrary")),
        name="out_proj",
    )(mix, w_out, x2, mod_base, mod_l)


def _router_kernel(x_ref, mb_ref, ml_ref, wr_ref, br_ref,
                   h_ref, idx_ref, gate_ref, rank_ref, cnt_ref, cnt_scr):
    i = pl.program_id(0)
    tm = x_ref.shape[0]

    @pl.when(i == 0)
    def _():
        cnt_scr[...] = jnp.zeros_like(cnt_scr)

    h = _norm_mod(x_ref[...], mb_ref, ml_ref, 3, 4)
    h_ref[...] = h
    logits = jnp.dot(h.astype(BF16), wr_ref[...].astype(BF16),
                     preferred_element_type=F32) + br_ref[...]
    lane = lax.broadcasted_iota(I32, (tm, LANES), 1)
    lane_f = lane.astype(F32)
    vals = jnp.where(lane < N_EXPERTS, logits, -jnp.inf)
    tops, sels = [], []
    idx_out = jnp.zeros((tm, LANES), F32)
    for k in range(TOP_K):
        m = jnp.max(vals, axis=-1, keepdims=True)
        first = jnp.min(jnp.where(vals == m, lane_f, float(LANES)), axis=-1, keepdims=True)
        sel = lane_f == first
        tops.append(m)
        sels.append(sel)
        idx_out = jnp.where(lane == k, first, idx_out)
        vals = jnp.where(sel, -jnp.inf, vals)
    exps = [jnp.exp(t - tops[0]) for t in tops]
    denom = exps[0] + exps[1] + exps[2] + exps[3]
    gate_out = jnp.zeros((tm, LANES), F32)
    for k in range(TOP_K):
        gate_out = jnp.where(lane == k, exps[k] / denom, gate_out)

    chosen = jnp.zeros((tm, LANES), F32)
    for sel in sels:
        chosen = chosen + sel.astype(F32)
    r = lax.broadcasted_iota(I32, (tm, tm), 0)
    c = lax.broadcasted_iota(I32, (tm, tm), 1)
    earlier = jnp.where(r > c, 1.0, 0.0).astype(BF16)
    before = jnp.dot(earlier, chosen.astype(BF16), preferred_element_type=F32) + cnt_scr[...]
    rank_out = jnp.zeros((tm, LANES), F32)
    for k in range(TOP_K):
        rk = jnp.sum(jnp.where(sels[k], before, 0.0), axis=-1, keepdims=True)
        rank_out = jnp.where(lane == k, rk, rank_out)
    cnt_scr[...] = cnt_scr[...] + jnp.sum(chosen, axis=0, keepdims=True)

    idx_ref[...] = idx_out.astype(I32)
    gate_ref[...] = gate_out
    rank_ref[...] = rank_out.astype(I32)
    cnt_ref[...] = cnt_scr[...].astype(I32)


def _ffn_norm_router(x2, mod_base, mod_l, w_router_p, b_router_p, S, tm=256):
    N, D = x2.shape
    per_b = S // tm
    tok_lane = pl.BlockSpec((tm, LANES), lambda i: (i, 0))
    return pl.pallas_call(
        _router_kernel,
        grid=(N // tm,),
        in_specs=[pl.BlockSpec((tm, D), lambda i: (i, 0)),
                  pl.BlockSpec((1, N_MOD, D), lambda i: (i // per_b, 0, 0)),
                  pl.BlockSpec((N_MOD, D), lambda i: (0, 0)),
                  pl.BlockSpec((D, LANES), lambda i: (0, 0)),
                  pl.BlockSpec((1, LANES), lambda i: (0, 0))],
        out_specs=[pl.BlockSpec((tm, D), lambda i: (i, 0)), tok_lane, tok_lane, tok_lane,
                   pl.BlockSpec((1, LANES), lambda i: (0, 0))],
        out_shape=[jax.ShapeDtypeStruct((N, D), F32),
                   jax.ShapeDtypeStruct((N, LANES), I32),
                   jax.ShapeDtypeStruct((N, LANES), F32),
                   jax.ShapeDtypeStruct((N, LANES), I32),
                   jax.ShapeDtypeStruct((1, LANES), I32)],
        scratch_shapes=[pltpu.VMEM((1, LANES), F32)],
        compiler_params=_params(("arbitrary",)),
        name="ffn_norm_router",
    )(x2, mod_base, mod_l, w_router_p, b_router_p)


def _expert_kernel(be_ref, nused_ref, stok_ref, h_hbm, wg_ref, wu_ref, wd_ref,
                   bg_ref, bu_ref, bd_ref, y_ref, xbuf, sem, *, tm):
    b = pl.program_id(0)
    nused = nused_ref[0]

    def row_copy(tok, slot, r):
        return pltpu.make_async_copy(h_hbm.at[pl.ds(tok, 1), :],
                                     xbuf.at[slot, pl.ds(r, 1), :], sem.at[slot])

    def issue(blk, slot):
        def body(r, carry):
            row_copy(stok_ref[blk * tm + r], slot, r).start()
            return carry
        lax.fori_loop(0, tm, body, 0)

    @pl.when(b == 0)
    def _():
        issue(0, 0)

    @pl.when(b + 1 < nused)
    def _():
        issue(b + 1, (b + 1) % 2)

    @pl.when(b < nused)
    def _():
        slot = b % 2
        pltpu.make_async_copy(h_hbm.at[pl.ds(0, tm), :], xbuf.at[slot], sem.at[slot]).wait()
        x = xbuf[slot].astype(BF16)
        gate = jnp.dot(x, wg_ref[0], preferred_element_type=F32) + bg_ref[0]
        up = jnp.dot(x, wu_ref[0], preferred_element_type=F32) + bu_ref[0]
        gate = jnp.minimum(gate, SWIGLU_LIMIT)
        up = jnp.clip(up, -SWIGLU_LIMIT, SWIGLU_LIMIT)
        act = (up + 1.0) * gate * jax.nn.sigmoid(SWIGLU_ALPHA * gate)
        y_ref[...] = jnp.dot(act.astype(BF16), wd_ref[0], preferred_element_type=F32) + bd_ref[0]

    @pl.when(b >= nused)
    def _():
        y_ref[...] = jnp.zeros_like(y_ref)


def _experts(h2, block_e, nused, slot_tok, wg, wu, wd, bg, bu, bd, tm):
    N, D = h2.shape
    n_blocks = block_e.shape[0]
    d_ff = wg.shape[2]
    e_map = lambda b, be, nu, st: (be[b], 0, 0)
    grid_spec = pltpu.PrefetchScalarGridSpec(
        num_scalar_prefetch=3,
        grid=(n_blocks,),
        in_specs=[pl.BlockSpec(memory_space=pl.ANY),
                  pl.BlockSpec((1, D, d_ff), e_map),
                  pl.BlockSpec((1, D, d_ff), e_map),
                  pl.BlockSpec((1, d_ff, D), e_map),
                  pl.BlockSpec((1, 1, d_ff), e_map),
                  pl.BlockSpec((1, 1, d_ff), e_map),
                  pl.BlockSpec((1, 1, D), e_map)],
        out_specs=pl.BlockSpec((tm, D), lambda b, be, nu, st: (b, 0)),
        scratch_shapes=[pltpu.VMEM((2, tm, D), F32), pltpu.SemaphoreType.DMA((2,))],
    )
    return pl.pallas_call(
        functools.partial(_expert_kernel, tm=tm),
        grid_spec=grid_spec,
        out_shape=jax.ShapeDtypeStruct((n_blocks * tm, D), F32),
        compiler_params=_params(("arbitrary",)),
        name="experts",
    )(block_e, nused, slot_tok, h2, wg, wu, wd, bg, bu, bd)


def _combine_kernel(dest_ref, y_hbm, x_ref, gate_ref, mb_ref, ml_ref, o_ref, buf, sem, *, tc):
    i = pl.program_id(0)
    n = pl.num_programs(0)

    def row_copy(d, slot, k, r):
        return pltpu.make_async_copy(y_hbm.at[pl.ds(d, 1), :],
                                     buf.at[slot, k, pl.ds(r, 1), :], sem.at[slot])

    def issue(step, slot):
        def body(r, carry):
            base = (step * tc + r) * TOP_K
            for k in range(TOP_K):
                row_copy(dest_ref[base + k], slot, k, r).start()
            return carry
        lax.fori_loop(0, tc, body, 0)

    @pl.when(i == 0)
    def _():
        issue(0, 0)

    @pl.when(i + 1 < n)
    def _():
        issue(i + 1, (i + 1) % 2)

    slot = i % 2
    for k in range(TOP_K):
        pltpu.make_async_copy(y_hbm.at[pl.ds(0, tc), :], buf.at[slot, k], sem.at[slot]).wait()
    acc = gate_ref[:, 0:1] * buf[slot, 0]
    for k in range(1, TOP_K):
        acc = acc + gate_ref[:, k:k + 1] * buf[slot, k]
    o_ref[...] = x_ref[...] + _mod_row(mb_ref, ml_ref, 5) * acc


def _combine(dest_flat, y, x2, gates, mod_base, mod_l, S, tc=128):
    N, D = x2.shape
    per_b = S // tc
    grid_spec = pltpu.PrefetchScalarGridSpec(
        num_scalar_prefetch=1,
        grid=(N // tc,),
        in_specs=[pl.BlockSpec(memory_space=pl.ANY),
                  pl.BlockSpec((tc, D), lambda i, d: (i, 0)),
                  pl.BlockSpec((tc, LANES), lambda i, d: (i, 0)),
                  pl.BlockSpec((1, N_MOD, D), lambda i, d: (i // per_b, 0, 0)),
                  pl.BlockSpec((N_MOD, D), lambda i, d: (0, 0))],
        out_specs=pl.BlockSpec((tc, D), lambda i, d: (i, 0)),
        scratch_shapes=[pltpu.VMEM((2, TOP_K, tc, D), F32), pltpu.SemaphoreType.DMA((2,))],
    )
    return pl.pallas_call(
        functools.partial(_combine_kernel, tc=tc),
        grid_spec=grid_spec,
        out_shape=jax.ShapeDtypeStruct((N, D), F32),
        compiler_params=_params(("arbitrary",)),
        name="moe_combine",
    )(dest_flat, y, x2, gates, mod_base, mod_l)


EXPERT_ROWS = 256


def _rope_tables(positions):
    half = ROT_DIM // 2
    inv_freq = ROPE_THETA ** (-jnp.arange(half, dtype=F32) * 2.0 / ROT_DIM)
    ang = positions.astype(F32).reshape(-1, 1) * inv_freq
    cos, sin = jnp.cos(ang), jnp.sin(ang)
    n = ang.shape[0]
    ones = jnp.ones((n, HEAD_DIM - ROT_DIM), F32)
    zeros = jnp.zeros((n, HEAD_DIM - half), F32)
    cos_t = jnp.concatenate([cos, cos, ones], axis=1)
    s1_t = jnp.concatenate([-sin, zeros], axis=1)
    s2_t = jnp.concatenate([jnp.zeros((n, half), F32), sin,
                            jnp.zeros((n, HEAD_DIM - ROT_DIM), F32)], axis=1)
    return cos_t, s1_t, s2_t


def _routing_tables(top_idx, rank, counts, tm, n_blocks):
    N = top_idx.shape[0]
    nblk = (counts + tm - 1) // tm
    blk_end = jnp.cumsum(nblk)
    blk_start = blk_end - nblk
    dest = blk_start[top_idx] * tm + rank
    block_e = jnp.minimum(jnp.searchsorted(blk_end, jnp.arange(n_blocks, dtype=I32), side='right'),
                          N_EXPERTS - 1).astype(I32)
    nused = blk_end[-1:].astype(I32)
    tok = jnp.broadcast_to(jnp.arange(N, dtype=I32)[:, None], (N, TOP_K))
    slot_tok = jnp.zeros((n_blocks * tm,), I32).at[dest.reshape(-1)].set(
        tok.reshape(-1), unique_indices=True)
    return dest.reshape(-1).astype(I32), block_e, nused, slot_tok


def kernel(x, c, positions, w_mod, b_mod, mod_layer, w_in, q_norm, k_norm, w_s, b_s, v_ln_g, v_ln_b,
           out_norm, w_out, w_router, b_router, w_gate_up, b_gate_up, w_down, b_down):
    B, S, D = x.shape
    N = B * S
    depth = w_in.shape[0]
    d_ff = w_down.shape[2]
    tm_e = EXPERT_ROWS
    n_blocks = N * TOP_K // tm_e + N_EXPERTS

    mod_base = _mod_base(c, w_mod, b_mod).reshape(B, N_MOD, D)
    cos_t, s1_t, s2_t = _rope_tables(positions)
    x2 = x.reshape(N, D)

    for l in range(depth):
        mod_l = mod_layer[l]
        h = _mixer_norm(x2, mod_base, mod_l, S)
        proj = _in_proj(h, w_in[l].astype(BF16), q_norm[l].reshape(1, HEAD_DIM),
                        k_norm[l].reshape(1, HEAD_DIM), cos_t, s1_t, s2_t,
                        v_ln_g[l].reshape(1, D_GMLP), v_ln_b[l].reshape(1, D_GMLP))
        attn = _attention(proj.reshape(B, S, -1))
        mix = _gmlp_and_norms(attn.reshape(N, D_ATTN), proj, w_s[l], jnp.transpose(b_s[l]),
                              out_norm[l].reshape(1, -1))
        x2 = _out_proj(mix, w_out[l].astype(BF16), x2, mod_base, mod_l, S)

        w_router_p = jnp.pad(w_router[l], ((0, 0), (0, LANES - N_EXPERTS)))
        b_router_p = jnp.pad(b_router[l], (0, LANES - N_EXPERTS)).reshape(1, LANES)
        h2, top_idx, gates, rank, counts = _ffn_norm_router(x2, mod_base, mod_l, w_router_p,
                                                            b_router_p, S)
        dest, block_e, nused, slot_tok = _routing_tables(
            top_idx[:, :TOP_K], rank[:, :TOP_K], counts[0, :N_EXPERTS], tm_e, n_blocks)
        wg = w_gate_up[l][:, :, 0::2].astype(BF16)
        wu = w_gate_up[l][:, :, 1::2].astype(BF16)
        bg = b_gate_up[l][:, 0::2].reshape(N_EXPERTS, 1, d_ff)
        bu = b_gate_up[l][:, 1::2].reshape(N_EXPERTS, 1, d_ff)
        y = _experts(h2, block_e, nused, slot_tok, wg, wu, w_down[l].astype(BF16), bg, bu,
                     b_down[l].reshape(N_EXPERTS, 1, D), tm_e)
        x2 = _combine(dest, y, x2, gates, mod_base, mod_l, S)
    return x2.reshape(B, S, D)
```

```python
import functools
import math

import jax
import jax.numpy as jnp
from jax import lax
from jax.experimental import pallas as pl
from jax.experimental.pallas import tpu as pltpu

F32 = jnp.float32
BF16 = jnp.bfloat16
I32 = jnp.int32

HEAD_DIM = 128
N_HEADS_ATTN = 16
N_GROUPS_GMLP = 16
D_ATTN = N_HEADS_ATTN * HEAD_DIM
D_GMLP = N_GROUPS_GMLP * HEAD_DIM
ROT_DIM = HEAD_DIM // 4
ROPE_THETA = 500000.0
DILATED_PATTERNS = ((128, 1), (512, 4), (2048, 16))
GMLP_CHUNK = 128
N_EXPERTS = 32
TOP_K = 4
SWIGLU_LIMIT = 7.0
SWIGLU_ALPHA = 1.702
N_MOD = 6
EPS = 1e-6
LANES = 128
MASKED = -1e30

VMEM_LIMIT = 56 * 1024 * 1024


def _params(sem, vmem=VMEM_LIMIT):
    return pltpu.CompilerParams(dimension_semantics=sem, vmem_limit_bytes=vmem)


def _mod_kernel(c_ref, w_ref, b_ref, o_ref):
    c = c_ref[...]
    s = c * jax.nn.sigmoid(c)
    o_ref[...] = jnp.dot(s.astype(BF16), w_ref[...].astype(BF16),
                         preferred_element_type=F32) + b_ref[...]


def _mod_base(c, w_mod, b_mod, tn=512):
    B, D = c.shape
    n_out = w_mod.shape[1]
    return pl.pallas_call(
        _mod_kernel,
        grid=(n_out // tn,),
        in_specs=[pl.BlockSpec((B, D), lambda j: (0, 0)),
                  pl.BlockSpec((D, tn), lambda j: (0, j)),
                  pl.BlockSpec((1, tn), lambda j: (0, j))],
        out_specs=pl.BlockSpec((B, tn), lambda j: (0, j)),
        out_shape=jax.ShapeDtypeStruct((B, n_out), F32),
        compiler_params=_params(("parallel",)),
        name="mod_base",
    )(c, w_mod, b_mod.reshape(1, n_out))


def _mod_row(mb_ref, ml_ref, idx):
    return mb_ref[0, idx:idx + 1, :] + ml_ref[idx:idx + 1, :]


def _norm_mod(x, mb_ref, ml_ref, shift_idx, scale_idx):
    y = x * lax.rsqrt(jnp.mean(x * x, axis=-1, keepdims=True) + EPS)
    return y * (1.0 + _mod_row(mb_ref, ml_ref, scale_idx)) + _mod_row(mb_ref, ml_ref, shift_idx)


def _norm_kernel(x_ref, mb_ref, ml_ref, h_ref):
    h_ref[...] = _norm_mod(x_ref[...], mb_ref, ml_ref, 0, 1).astype(h_ref.dtype)


def _mixer_norm(x2, mod_base, mod_l, S, tm=256):
    N, D = x2.shape
    per_b = S // tm
    return pl.pallas_call(
        _norm_kernel,
        grid=(N // tm,),
        in_specs=[pl.BlockSpec((tm, D), lambda i: (i, 0)),
                  pl.BlockSpec((1, N_MOD, D), lambda i: (i // per_b, 0, 0)),
                  pl.BlockSpec((N_MOD, D), lambda i: (0, 0))],
        out_specs=pl.BlockSpec((tm, D), lambda i: (i, 0)),
        out_shape=jax.ShapeDtypeStruct((N, D), BF16),
        compiler_params=_params(("parallel",)),
        name="mixer_norm",
    )(x2, mod_base, mod_l)


def _gelu(x):
    return 0.5 * x * (1.0 + lax.erf(x * (1.0 / math.sqrt(2.0))))


def _proj_kernel(*refs, mode, sub):
    if mode == "qk":
        h_ref, w_ref, gain_ref, cos_ref, s1_ref, s2_ref, o_ref = refs
    elif mode == "gelu_ln":
        h_ref, w_ref, lg_ref, lb_ref, o_ref = refs
    else:
        h_ref, w_ref, o_ref = refs
    bn = o_ref.shape[1]
    for s in range(bn // sub):
        acc = jnp.dot(h_ref[...], w_ref[:, s * sub:(s + 1) * sub], preferred_element_type=F32)
        for hh in range(sub // HEAD_DIM):
            t = acc[:, hh * HEAD_DIM:(hh + 1) * HEAD_DIM]
            sl = slice(s * sub + hh * HEAD_DIM, s * sub + (hh + 1) * HEAD_DIM)
            if mode == "qk":
                y = t * lax.rsqrt(jnp.mean(t * t, axis=-1, keepdims=True) + EPS) * gain_ref[0]
                r = (y * cos_ref[...] + pltpu.roll(y, HEAD_DIM - ROT_DIM // 2, 1) * s1_ref[...]
                     + pltpu.roll(y, ROT_DIM // 2, 1) * s2_ref[...])
            elif mode == "gelu":
                r = _gelu(t)
            elif mode == "gelu_ln":
                g = _gelu(t)
                gc = g - jnp.mean(g, axis=-1, keepdims=True)
                r = gc * lax.rsqrt(jnp.mean(gc * gc, axis=-1, keepdims=True) + EPS)
                r = r * lg_ref[:, sl] + lb_ref[:, sl]
            else:
                r = t
            o_ref[:, sl] = r.astype(o_ref.dtype)


def _proj(h, w, col0, width, mode, extra=(), extra_specs=(), bm=1024, bn=1024, sub=256):
    N, D = h.shape
    j0 = col0 // bn
    return pl.pallas_call(
        functools.partial(_proj_kernel, mode=mode, sub=sub),
        grid=(N // bm, width // bn),
        in_specs=[pl.BlockSpec((bm, D), lambda i, j: (i, 0)),
                  pl.BlockSpec((D, bn), lambda i, j: (0, j0 + j))] + list(extra_specs),
        out_specs=pl.BlockSpec((bm, bn), lambda i, j: (i, j)),
        out_shape=jax.ShapeDtypeStruct((N, width), BF16),
        compiler_params=_params(("parallel", "arbitrary")),
        name="in_proj_" + mode,
    )(h, w, *extra)


def _in_proj(h, w_in, qk_gain, cos_t, s1_t, s2_t, ln_g, ln_b, bm=1024, bn=1024):
    tiles_per_part = D_ATTN // bn
    tok_tab = pl.BlockSpec((bm, HEAD_DIM), lambda i, j: (i, 0))
    qk = _proj(h, w_in, 0, 2 * D_ATTN, "qk", (qk_gain, cos_t, s1_t, s2_t),
               (pl.BlockSpec((1, 1, HEAD_DIM), lambda i, j: (j // tiles_per_part, 0, 0)),
                tok_tab, tok_tab, tok_tab), bm, bn)
    v = _proj(h, w_in, 2 * D_ATTN, D_ATTN, "none", bm=bm, bn=bn)
    u = _proj(h, w_in, 3 * D_ATTN, D_GMLP, "gelu", bm=bm, bn=bn)
    row = pl.BlockSpec((1, bn), lambda i, j: (0, j))
    gv = _proj(h, w_in, 3 * D_ATTN + D_GMLP, D_GMLP, "gelu_ln", (ln_g, ln_b), (row, row), bm, bn)
    return qk, v, u, gv


def _attn_kernel(q_ref, k_ref, v_ref, bias_ref, o_ref, *, tq):
    S = q_ref.shape[1]
    n_q = S // tq
    for i in range(n_q):
        kend = (i + 1) * tq
        q = q_ref[0, i * tq:(i + 1) * tq, :]
        s = lax.dot_general(q, k_ref[0, :kend, :], (((1,), (1,)), ((), ())),
                            preferred_element_type=F32)
        off = (n_q - 1 - i) * tq
        s = s + bias_ref[:, off:off + kend]
        m = jnp.max(s, axis=-1, keepdims=True)
        p = jnp.exp(s - m)
        l = jnp.sum(p, axis=-1, keepdims=True)
        o = jnp.dot(p.astype(BF16), v_ref[0, :kend, :], preferred_element_type=F32)
        o_ref[0, i * tq:(i + 1) * tq, :] = (o / l).astype(o_ref.dtype)


def _attn_bias_table(S, tq):
    r = jnp.arange(tq, dtype=I32)[:, None]
    u = jnp.arange(S, dtype=I32)[None, :]
    d = r + (S - tq) - u
    mult = jnp.zeros((tq, S), F32)
    for window, dil in DILATED_PATTERNS:
        mult = mult + ((d >= 0) & (d % dil == 0) & (d <= window)).astype(F32)
    return jnp.where(mult > 0, jnp.log(jnp.maximum(mult, 1.0)), MASKED)


def _attention(qk3, v3, tq=256):
    B, S, _ = v3.shape
    bias = _attn_bias_table(S, tq)
    H = N_HEADS_ATTN
    return pl.pallas_call(
        functools.partial(_attn_kernel, tq=tq),
        grid=(B, H),
        in_specs=[pl.BlockSpec((1, S, HEAD_DIM), lambda b, h: (b, 0, h)),
                  pl.BlockSpec((1, S, HEAD_DIM), lambda b, h: (b, 0, H + h)),
                  pl.BlockSpec((1, S, HEAD_DIM), lambda b, h: (b, 0, h)),
                  pl.BlockSpec((tq, S), lambda b, h: (0, 0))],
        out_specs=pl.BlockSpec((1, S, HEAD_DIM), lambda b, h: (b, 0, h)),
        out_shape=jax.ShapeDtypeStruct((B, S, D_ATTN), BF16),
        compiler_params=_params(("parallel", "parallel")),
        name="dilated_attn",
    )(qk3, qk3, v3, bias)


def _gmlp_kernel(a_ref, u_ref, gv_ref, ws_ref, bst_ref, on_ref, o_ref, g_scr):
    tg = a_ref.shape[0]
    row = lax.broadcasted_iota(I32, (GMLP_CHUNK, GMLP_CHUNK), 0)
    col = lax.broadcasted_iota(I32, (GMLP_CHUNK, GMLP_CHUNK), 1)
    causal = row >= col
    for g in range(N_GROUPS_GMLP):
        w = jnp.where(causal, ws_ref[g], 0.0).astype(BF16)
        bias = bst_ref[:, g:g + 1]
        sl = slice(g * HEAD_DIM, (g + 1) * HEAD_DIM)
        for c in range(tg // GMLP_CHUNK):
            rows = slice(c * GMLP_CHUNK, (c + 1) * GMLP_CHUNK)
            mixed = jnp.dot(w, gv_ref[rows, sl], preferred_element_type=F32) + bias
            g_scr[rows, sl] = u_ref[rows, sl].astype(F32) * mixed
    gm = g_scr[...]
    gm = gm * lax.rsqrt(jnp.mean(gm * gm, axis=-1, keepdims=True) + EPS)
    o_ref[:, D_ATTN:] = (gm * on_ref[:, D_ATTN:]).astype(o_ref.dtype)
    a = a_ref[...].astype(F32)
    a = a * lax.rsqrt(jnp.mean(a * a, axis=-1, keepdims=True) + EPS)
    o_ref[:, :D_ATTN] = (a * on_ref[:, :D_ATTN]).astype(o_ref.dtype)


def _gmlp_and_norms(attn2, u, gv, w_s, b_s_t, out_norm, tg=512):
    N = attn2.shape[0]
    D_mix = D_ATTN + D_GMLP
    return pl.pallas_call(
        _gmlp_kernel,
        grid=(N // tg,),
        in_specs=[pl.BlockSpec((tg, D_ATTN), lambda i: (i, 0)),
                  pl.BlockSpec((tg, D_GMLP), lambda i: (i, 0)),
                  pl.BlockSpec((tg, D_GMLP), lambda i: (i, 0)),
                  pl.BlockSpec((N_GROUPS_GMLP, GMLP_CHUNK, GMLP_CHUNK), lambda i: (0, 0, 0)),
                  pl.BlockSpec((GMLP_CHUNK, N_GROUPS_GMLP), lambda i: (0, 0)),
                  pl.BlockSpec((1, D_mix), lambda i: (0, 0))],
        out_specs=pl.BlockSpec((tg, D_mix), lambda i: (i, 0)),
        out_shape=jax.ShapeDtypeStruct((N, D_mix), BF16),
        scratch_shapes=[pltpu.VMEM((tg, D_GMLP), F32)],
        compiler_params=_params(("parallel",)),
        name="gmlp_norms",
    )(attn2, u, gv, w_s, b_s_t, out_norm)


def _outproj_kernel(m_ref, w_ref, x_ref, mb_ref, ml_ref, o_ref):
    acc = jnp.dot(m_ref[...], w_ref[...], preferred_element_type=F32)
    o_ref[...] = x_ref[...] + _mod_row(mb_ref, ml_ref, 2) * acc


def _out_proj(mix, w_out, x2, mod_base, mod_l, S, bm=1024, bn=512):
    N, D_mix = mix.shape
    D = w_out.shape[1]
    per_b = S // bm
    return pl.pallas_call(
        _outproj_kernel,
        grid=(N // bm, D // bn),
        in_specs=[pl.BlockSpec((bm, D_mix), lambda i, j: (i, 0)),
                  pl.BlockSpec((D_mix, bn), lambda i, j: (0, j)),
                  pl.BlockSpec((bm, bn), lambda i, j: (i, j)),
                  pl.BlockSpec((1, N_MOD, bn), lambda i, j: (i // per_b, 0, j)),
                  pl.BlockSpec((N_MOD, bn), lambda i, j: (0, j))],
        out_specs=pl.BlockSpec((bm, bn), lambda i, j: (i, j)),
        out_shape=jax.ShapeDtypeStruct((N, D), F32),
        compiler_params=_params(("parallel", "arbitrary")),
        name="out_proj",
    )(mix, w_out, x2, mod_base, mod_l)


def _router_kernel(x_ref, mb_ref, ml_ref, wr_ref, br_ref,
                   h_ref, idx_ref, gate_ref, rank_ref, cnt_ref, cnt_scr):
    i = pl.program_id(0)
    tm = x_ref.shape[0]

    @pl.when(i == 0)
    def _():
        cnt_scr[...] = jnp.zeros_like(cnt_scr)

    h = _norm_mod(x_ref[...], mb_ref, ml_ref, 3, 4)
    h_ref[...] = h
    logits = jnp.dot(h.astype(BF16), wr_ref[...].astype(BF16),
                     preferred_element_type=F32) + br_ref[...]
    lane = lax.broadcasted_iota(I32, (tm, LANES), 1)
    lane_f = lane.astype(F32)
    vals = jnp.where(lane < N_EXPERTS, logits, -jnp.inf)
    tops, sels = [], []
    idx_out = jnp.zeros((tm, LANES), F32)
    for k in range(TOP_K):
        m = jnp.max(vals, axis=-1, keepdims=True)
        first = jnp.min(jnp.where(vals == m, lane_f, float(LANES)), axis=-1, keepdims=True)
        sel = lane_f == first
        tops.append(m)
        sels.append(sel)
        idx_out = jnp.where(lane == k, first, idx_out)
        vals = jnp.where(sel, -jnp.inf, vals)
    exps = [jnp.exp(t - tops[0]) for t in tops]
    denom = exps[0] + exps[1] + exps[2] + exps[3]
    gate_out = jnp.zeros((tm, LANES), F32)
    for k in range(TOP_K):
        gate_out = jnp.where(lane == k, exps[k] / denom, gate_out)

    chosen = jnp.zeros((tm, LANES), F32)
    for sel in sels:
        chosen = chosen + sel.astype(F32)
    r = lax.broadcasted_iota(I32, (tm, tm), 0)
    c = lax.broadcasted_iota(I32, (tm, tm), 1)
    earlier = jnp.where(r > c, 1.0, 0.0).astype(BF16)
    before = jnp.dot(earlier, chosen.astype(BF16), preferred_element_type=F32) + cnt_scr[...]
    rank_out = jnp.zeros((tm, LANES), F32)
    for k in range(TOP_K):
        rk = jnp.sum(jnp.where(sels[k], before, 0.0), axis=-1, keepdims=True)
        rank_out = jnp.where(lane == k, rk, rank_out)
    cnt_scr[...] = cnt_scr[...] + jnp.sum(chosen, axis=0, keepdims=True)

    idx_ref[...] = idx_out.astype(I32)
    gate_ref[...] = gate_out
    rank_ref[...] = rank_out.astype(I32)
    cnt_ref[...] = cnt_scr[...].astype(I32)


def _ffn_norm_router(x2, mod_base, mod_l, w_router_p, b_router_p, S, tm=256):
    N, D = x2.shape
    per_b = S // tm
    tok_lane = pl.BlockSpec((tm, LANES), lambda i: (i, 0))
    return pl.pallas_call(
        _router_kernel,
        grid=(N // tm,),
        in_specs=[pl.BlockSpec((tm, D), lambda i: (i, 0)),
                  pl.BlockSpec((1, N_MOD, D), lambda i: (i // per_b, 0, 0)),
                  pl.BlockSpec((N_MOD, D), lambda i: (0, 0)),
                  pl.BlockSpec((D, LANES), lambda i: (0, 0)),
                  pl.BlockSpec((1, LANES), lambda i: (0, 0))],
        out_specs=[pl.BlockSpec((tm, D), lambda i: (i, 0)), tok_lane, tok_lane, tok_lane,
                   pl.BlockSpec((1, LANES), lambda i: (0, 0))],
        out_shape=[jax.ShapeDtypeStruct((N, D), F32),
                   jax.ShapeDtypeStruct((N, LANES), I32),
                   jax.ShapeDtypeStruct((N, LANES), F32),
                   jax.ShapeDtypeStruct((N, LANES), I32),
                   jax.ShapeDtypeStruct((1, LANES), I32)],
        scratch_shapes=[pltpu.VMEM((1, LANES), F32)],
        compiler_params=_params(("arbitrary",)),
        name="ffn_norm_router",
    )(x2, mod_base, mod_l, w_router_p, b_router_p)


MXU_DIM = 256


def _split_gate_up_kernel(w_ref, g_ref, u_ref):
    r = lax.broadcasted_iota(I32, (MXU_DIM, MXU_DIM), 0)
    c = lax.broadcasted_iota(I32, (MXU_DIM, MXU_DIM), 1)
    src = jnp.where(c < LANES, 2 * c, 2 * (c - LANES) + 1)
    perm = jnp.where(r == src, 1.0, 0.0).astype(BF16)
    for ch in range(w_ref.shape[3] // MXU_DIM):
        w = w_ref[0, 0, :, ch * MXU_DIM:(ch + 1) * MXU_DIM].astype(BF16)
        res = jnp.dot(w, perm, preferred_element_type=F32)
        g_ref[0, :, ch * LANES:(ch + 1) * LANES] = res[:, :LANES].astype(BF16)
        u_ref[0, :, ch * LANES:(ch + 1) * LANES] = res[:, LANES:].astype(BF16)


def _split_gate_up(w_gate_up, layer, tr=1024):
    _, E, D, two_ff = w_gate_up.shape
    d_ff = two_ff // 2
    out = jax.ShapeDtypeStruct((E, D, d_ff), BF16)
    return pl.pallas_call(
        _split_gate_up_kernel,
        grid=(E, D // tr),
        in_specs=[pl.BlockSpec((1, 1, tr, two_ff), lambda e, i: (layer, e, i, 0))],
        out_specs=[pl.BlockSpec((1, tr, d_ff), lambda e, i: (e, i, 0)),
                   pl.BlockSpec((1, tr, d_ff), lambda e, i: (e, i, 0))],
        out_shape=[out, out],
        compiler_params=_params(("parallel", "parallel")),
        name="split_gate_up",
    )(w_gate_up)


def _expert_kernel(be_ref, nused_ref, stok_ref, h_hbm, wg_ref, wu_ref, wd_ref,
                   bg_ref, bu_ref, bd_ref, y_ref, xbuf, sem, *, tm):
    b = pl.program_id(0)
    nused = nused_ref[0]

    def row_copy(tok, slot, r):
        return pltpu.make_async_copy(h_hbm.at[pl.ds(tok, 1), :],
                                     xbuf.at[slot, pl.ds(r, 1), :], sem.at[slot])

    def issue(blk, slot):
        def body(r, carry):
            row_copy(stok_ref[blk * tm + r], slot, r).start()
            return carry
        lax.fori_loop(0, tm, body, 0, unroll=8)

    @pl.when(b == 0)
    def _():
        issue(0, 0)

    @pl.when(b + 1 < nused)
    def _():
        issue(b + 1, (b + 1) % 2)

    @pl.when(b < nused)
    def _():
        slot = b % 2
        pltpu.make_async_copy(h_hbm.at[pl.ds(0, tm), :], xbuf.at[slot], sem.at[slot]).wait()
        x = xbuf[slot].astype(BF16)
        gate = jnp.dot(x, wg_ref[0], preferred_element_type=F32) + bg_ref[0]
        up = jnp.dot(x, wu_ref[0], preferred_element_type=F32) + bu_ref[0]
        gate = jnp.minimum(gate, SWIGLU_LIMIT)
        up = jnp.clip(up, -SWIGLU_LIMIT, SWIGLU_LIMIT)
        act = (up + 1.0) * gate * jax.nn.sigmoid(SWIGLU_ALPHA * gate)
        y_ref[...] = jnp.dot(act.astype(BF16), wd_ref[0], preferred_element_type=F32) + bd_ref[0]

    @pl.when(b >= nused)
    def _():
        y_ref[...] = jnp.zeros_like(y_ref)


def _experts(h2, block_e, nused, slot_tok, wg, wu, wd, bg, bu, bd, tm):
    N, D = h2.shape
    n_blocks = block_e.shape[0]
    d_ff = wg.shape[2]
    e_map = lambda b, be, nu, st: (be[b], 0, 0)
    grid_spec = pltpu.PrefetchScalarGridSpec(
        num_scalar_prefetch=3,
        grid=(n_blocks,),
        in_specs=[pl.BlockSpec(memory_space=pl.ANY),
                  pl.BlockSpec((1, D, d_ff), e_map),
                  pl.BlockSpec((1, D, d_ff), e_map),
                  pl.BlockSpec((1, d_ff, D), e_map),
                  pl.BlockSpec((1, 1, d_ff), e_map),
                  pl.BlockSpec((1, 1, d_ff), e_map),
                  pl.BlockSpec((1, 1, D), e_map)],
        out_specs=pl.BlockSpec((tm, D), lambda b, be, nu, st: (b, 0)),
        scratch_shapes=[pltpu.VMEM((2, tm, D), F32), pltpu.SemaphoreType.DMA((2,))],
    )
    return pl.pallas_call(
        functools.partial(_expert_kernel, tm=tm),
        grid_spec=grid_spec,
        out_shape=jax.ShapeDtypeStruct((n_blocks * tm, D), F32),
        compiler_params=_params(("arbitrary",)),
        name="experts",
    )(block_e, nused, slot_tok, h2, wg, wu, wd, bg, bu, bd)


def _combine_kernel(dest_ref, y_hbm, x_ref, gate_ref, mb_ref, ml_ref, o_ref, buf, sem, *, tc):
    i = pl.program_id(0)
    n = pl.num_programs(0)

    def row_copy(d, slot, k, r):
        return pltpu.make_async_copy(y_hbm.at[pl.ds(d, 1), :],
                                     buf.at[slot, k, pl.ds(r, 1), :], sem.at[slot])

    def issue(step, slot):
        def body(r, carry):
            base = (step * tc + r) * TOP_K
            for k in range(TOP_K):
                row_copy(dest_ref[base + k], slot, k, r).start()
            return carry
        lax.fori_loop(0, tc, body, 0, unroll=4)

    @pl.when(i == 0)
    def _():
        issue(0, 0)

    @pl.when(i + 1 < n)
    def _():
        issue(i + 1, (i + 1) % 2)

    slot = i % 2
    for k in range(TOP_K):
        pltpu.make_async_copy(y_hbm.at[pl.ds(0, tc), :], buf.at[slot, k], sem.at[slot]).wait()
    acc = gate_ref[:, 0:1] * buf[slot, 0]
    for k in range(1, TOP_K):
        acc = acc + gate_ref[:, k:k + 1] * buf[slot, k]
    o_ref[...] = x_ref[...] + _mod_row(mb_ref, ml_ref, 5) * acc


def _combine(dest_flat, y, x2, gates, mod_base, mod_l, S, tc=128):
    N, D = x2.shape
    per_b = S // tc
    grid_spec = pltpu.PrefetchScalarGridSpec(
        num_scalar_prefetch=1,
        grid=(N // tc,),
        in_specs=[pl.BlockSpec(memory_space=pl.ANY),
                  pl.BlockSpec((tc, D), lambda i, d: (i, 0)),
                  pl.BlockSpec((tc, LANES), lambda i, d: (i, 0)),
                  pl.BlockSpec((1, N_MOD, D), lambda i, d: (i // per_b, 0, 0)),
                  pl.BlockSpec((N_MOD, D), lambda i, d: (0, 0))],
        out_specs=pl.BlockSpec((tc, D), lambda i, d: (i, 0)),
        scratch_shapes=[pltpu.VMEM((2, TOP_K, tc, D), F32), pltpu.SemaphoreType.DMA((2,))],
    )
    return pl.pallas_call(
        functools.partial(_combine_kernel, tc=tc),
        grid_spec=grid_spec,
        out_shape=jax.ShapeDtypeStruct((N, D), F32),
        compiler_params=_params(("arbitrary",)),
        name="moe_combine",
    )(dest_flat, y, x2, gates, mod_base, mod_l)


EXPERT_ROWS = 256


def _rope_tables(positions):
    half = ROT_DIM // 2
    inv_freq = ROPE_THETA ** (-jnp.arange(half, dtype=F32) * 2.0 / ROT_DIM)
    ang = positions.astype(F32).reshape(-1, 1) * inv_freq
    cos, sin = jnp.cos(ang), jnp.sin(ang)
    n = ang.shape[0]
    ones = jnp.ones((n, HEAD_DIM - ROT_DIM), F32)
    zeros = jnp.zeros((n, HEAD_DIM - half), F32)
    cos_t = jnp.concatenate([cos, cos, ones], axis=1)
    s1_t = jnp.concatenate([-sin, zeros], axis=1)
    s2_t = jnp.concatenate([jnp.zeros((n, half), F32), sin,
                            jnp.zeros((n, HEAD_DIM - ROT_DIM), F32)], axis=1)
    return cos_t, s1_t, s2_t


def _routing_tables(top_idx, rank, counts, tm, n_blocks):
    N = top_idx.shape[0]
    nblk = (counts + tm - 1) // tm
    blk_end = jnp.cumsum(nblk)
    blk_start = blk_end - nblk
    dest = blk_start[top_idx] * tm + rank
    owner = jnp.sum((blk_end[None, :] <= jnp.arange(n_blocks, dtype=I32)[:, None]).astype(I32), axis=1)
    block_e = jnp.minimum(owner, N_EXPERTS - 1).astype(I32)
    nused = blk_end[-1:].astype(I32)
    tok = jnp.broadcast_to(jnp.arange(N, dtype=I32)[:, None], (N, TOP_K))
    slot_tok = jnp.zeros((n_blocks * tm,), I32).at[dest.reshape(-1)].set(
        tok.reshape(-1), unique_indices=True)
    return dest.reshape(-1).astype(I32), block_e, nused, slot_tok


def kernel(x, c, positions, w_mod, b_mod, mod_layer, w_in, q_norm, k_norm, w_s, b_s, v_ln_g, v_ln_b,
           out_norm, w_out, w_router, b_router, w_gate_up, b_gate_up, w_down, b_down):
    B, S, D = x.shape
    N = B * S
    depth = w_in.shape[0]
    d_ff = w_down.shape[2]
    tm_e = EXPERT_ROWS
    n_blocks = N * TOP_K // tm_e + N_EXPERTS

    mod_base = _mod_base(c, w_mod, b_mod).reshape(B, N_MOD, D)
    cos_t, s1_t, s2_t = _rope_tables(positions)
    x2 = x.reshape(N, D)

    for l in range(depth):
        mod_l = mod_layer[l]
        h = _mixer_norm(x2, mod_base, mod_l, S)
        qk_gain = jnp.stack([q_norm[l] * HEAD_DIM ** -0.5, k_norm[l]]).reshape(2, 1, HEAD_DIM)
        qk, v, u, gv = _in_proj(h, w_in[l].astype(BF16), qk_gain, cos_t, s1_t, s2_t,
                                v_ln_g[l].reshape(1, D_GMLP), v_ln_b[l].reshape(1, D_GMLP))
        attn = _attention(qk.reshape(B, S, -1), v.reshape(B, S, -1))
        mix = _gmlp_and_norms(attn.reshape(N, D_ATTN), u, gv, w_s[l], jnp.transpose(b_s[l]),
                              out_norm[l].reshape(1, -1))
        x2 = _out_proj(mix, w_out[l].astype(BF16), x2, mod_base, mod_l, S)

        w_router_p = jnp.pad(w_router[l], ((0, 0), (0, LANES - N_EXPERTS)))
        b_router_p = jnp.pad(b_router[l], (0, LANES - N_EXPERTS)).reshape(1, LANES)
        h2, top_idx, gates, rank, counts = _ffn_norm_router(x2, mod_base, mod_l, w_router_p,
                                                            b_router_p, S)
        dest, block_e, nused, slot_tok = _routing_tables(
            top_idx[:, :TOP_K], rank[:, :TOP_K], counts[0, :N_EXPERTS], tm_e, n_blocks)
        wg, wu = _split_gate_up(w_gate_up, l)
        bg = b_gate_up[l][:, 0::2].reshape(N_EXPERTS, 1, d_ff)
        bu = b_gate_up[l][:, 1::2].reshape(N_EXPERTS, 1, d_ff)
        y = _experts(h2, block_e, nused, slot_tok, wg, wu, w_down[l].astype(BF16), bg, bu,
                     b_down[l].reshape(N_EXPERTS, 1, D), tm_e)
        x2 = _combine(dest, y, x2, gates, mod_base, mod_l, S)
    return x2.reshape(B, S, D)
```

```python
import functools
import math

import jax
import jax.numpy as jnp
from jax import lax
from jax.experimental import pallas as pl
from jax.experimental.pallas import tpu as pltpu

F32 = jnp.float32
BF16 = jnp.bfloat16
I32 = jnp.int32

HEAD_DIM = 128
N_HEADS_ATTN = 16
N_GROUPS_GMLP = 16
D_ATTN = N_HEADS_ATTN * HEAD_DIM
D_GMLP = N_GROUPS_GMLP * HEAD_DIM
ROT_DIM = HEAD_DIM // 4
ROPE_THETA = 500000.0
DILATED_PATTERNS = ((128, 1), (512, 4), (2048, 16))
GMLP_CHUNK = 128
N_EXPERTS = 32
TOP_K = 4
SWIGLU_LIMIT = 7.0
SWIGLU_ALPHA = 1.702
N_MOD = 6
EPS = 1e-6
LANES = 128
MASKED = -1e30

VMEM_LIMIT = 56 * 1024 * 1024


def _params(sem, vmem=VMEM_LIMIT):
    return pltpu.CompilerParams(dimension_semantics=sem, vmem_limit_bytes=vmem)


def _mod_kernel(c_ref, w_ref, b_ref, o_ref):
    c = c_ref[...]
    s = c * jax.nn.sigmoid(c)
    o_ref[...] = jnp.dot(s.astype(BF16), w_ref[...].astype(BF16),
                         preferred_element_type=F32) + b_ref[...]


def _mod_base(c, w_mod, b_mod, tn=512):
    B, D = c.shape
    n_out = w_mod.shape[1]
    return pl.pallas_call(
        _mod_kernel,
        grid=(n_out // tn,),
        in_specs=[pl.BlockSpec((B, D), lambda j: (0, 0)),
                  pl.BlockSpec((D, tn), lambda j: (0, j)),
                  pl.BlockSpec((1, tn), lambda j: (0, j))],
        out_specs=pl.BlockSpec((B, tn), lambda j: (0, j)),
        out_shape=jax.ShapeDtypeStruct((B, n_out), F32),
        compiler_params=_params(("parallel",)),
        name="mod_base",
    )(c, w_mod, b_mod.reshape(1, n_out))


def _pack_halves(x):
    c = x.shape[1] // 2
    bits = pltpu.bitcast(x.astype(BF16).astype(F32), jnp.uint32)
    return (bits[:, c:] & jnp.uint32(0xFFFF0000)) | (bits[:, :c] >> 16)


def _unpack_halves(p):
    lo = pltpu.bitcast(p << 16, F32)
    hi = pltpu.bitcast(p & jnp.uint32(0xFFFF0000), F32)
    return lo, hi


def _mod_row(mb_ref, ml_ref, idx):
    return mb_ref[0, idx:idx + 1, :] + ml_ref[idx:idx + 1, :]


def _norm_mod(x, mb_ref, ml_ref, shift_idx, scale_idx):
    y = x * lax.rsqrt(jnp.mean(x * x, axis=-1, keepdims=True) + EPS)
    return y * (1.0 + _mod_row(mb_ref, ml_ref, scale_idx)) + _mod_row(mb_ref, ml_ref, shift_idx)


def _norm_kernel(x_ref, mb_ref, ml_ref, h_ref):
    h_ref[...] = _norm_mod(x_ref[...], mb_ref, ml_ref, 0, 1).astype(h_ref.dtype)


def _mixer_norm(x2, mod_base, mod_l, S, tm=256):
    N, D = x2.shape
    per_b = S // tm
    return pl.pallas_call(
        _norm_kernel,
        grid=(N // tm,),
        in_specs=[pl.BlockSpec((tm, D), lambda i: (i, 0)),
                  pl.BlockSpec((1, N_MOD, D), lambda i: (i // per_b, 0, 0)),
                  pl.BlockSpec((N_MOD, D), lambda i: (0, 0))],
        out_specs=pl.BlockSpec((tm, D), lambda i: (i, 0)),
        out_shape=jax.ShapeDtypeStruct((N, D), BF16),
        compiler_params=_params(("parallel",)),
        name="mixer_norm",
    )(x2, mod_base, mod_l)


def _gelu(x):
    return 0.5 * x * (1.0 + lax.erf(x * (1.0 / math.sqrt(2.0))))


def _proj_kernel(*refs, mode, sub):
    if mode == "gelu_ln":
        h_ref, w_ref, lg_ref, lb_ref, o_ref = refs
    else:
        h_ref, w_ref, o_ref = refs
    bn = o_ref.shape[1]
    for s in range(bn // sub):
        acc = jnp.dot(h_ref[...], w_ref[:, s * sub:(s + 1) * sub], preferred_element_type=F32)
        for hh in range(sub // HEAD_DIM):
            t = acc[:, hh * HEAD_DIM:(hh + 1) * HEAD_DIM]
            sl = slice(s * sub + hh * HEAD_DIM, s * sub + (hh + 1) * HEAD_DIM)
            if mode == "gelu":
                r = _gelu(t)
            elif mode == "gelu_ln":
                g = _gelu(t)
                gc = g - jnp.mean(g, axis=-1, keepdims=True)
                r = gc * lax.rsqrt(jnp.mean(gc * gc, axis=-1, keepdims=True) + EPS)
                r = r * lg_ref[:, sl] + lb_ref[:, sl]
            else:
                r = t
            o_ref[:, sl] = r.astype(o_ref.dtype)


def _proj(h, w, col0, width, mode, extra=(), extra_specs=(), bm=1024, bn=1024, sub=256):
    N, D = h.shape
    j0 = col0 // bn
    return pl.pallas_call(
        functools.partial(_proj_kernel, mode=mode, sub=sub),
        grid=(N // bm, width // bn),
        in_specs=[pl.BlockSpec((bm, D), lambda i, j: (i, 0)),
                  pl.BlockSpec((D, bn), lambda i, j: (0, j0 + j))] + list(extra_specs),
        out_specs=pl.BlockSpec((bm, bn), lambda i, j: (i, j)),
        out_shape=jax.ShapeDtypeStruct((N, width), BF16),
        compiler_params=_params(("parallel", "arbitrary")),
        name="in_proj_" + mode,
    )(h, w, *extra)


def _qk_proj_kernel(h_ref, w_ref, gain_ref, cos_ref, s1_ref, s2_ref, o_ref, acc0, acc1):
    t = pl.program_id(0)
    bn = o_ref.shape[1]

    @pl.when(t == 0)
    def _():
        acc1[...] = jnp.zeros_like(acc1)

    def step(acc_mm, acc_ep):
        acc_mm[...] = jnp.dot(h_ref[...], w_ref[...], preferred_element_type=F32)
        for hh in range(bn // HEAD_DIM):
            sl = slice(hh * HEAD_DIM, (hh + 1) * HEAD_DIM)
            a = acc_ep[:, sl]
            y = a * lax.rsqrt(jnp.mean(a * a, axis=-1, keepdims=True) + EPS) * gain_ref[0]
            r = (y * cos_ref[...] + pltpu.roll(y, HEAD_DIM - ROT_DIM // 2, 1) * s1_ref[...]
                 + pltpu.roll(y, ROT_DIM // 2, 1) * s2_ref[...])
            o_ref[:, sl] = r.astype(o_ref.dtype)

    @pl.when(t % 2 == 0)
    def _():
        step(acc0, acc1)

    @pl.when(t % 2 == 1)
    def _():
        step(acc1, acc0)


def _qk_proj(h, w, qk_gain, cos_t, s1_t, s2_t, bm=1024, bn=512):
    N, D = h.shape
    width = 2 * D_ATTN
    nj = width // bn
    n_tiles = (N // bm) * nj
    tiles_per_part = D_ATTN // bn
    mm = lambda t: jnp.minimum(t, n_tiles - 1)
    ep = lambda t: jnp.maximum(t - 1, 0)
    tok_tab = pl.BlockSpec((bm, HEAD_DIM), lambda t: (ep(t) // nj, 0))
    return pl.pallas_call(
        _qk_proj_kernel,
        grid=(n_tiles + 1,),
        in_specs=[pl.BlockSpec((bm, D), lambda t: (mm(t) // nj, 0)),
                  pl.BlockSpec((D, bn), lambda t: (0, mm(t) % nj)),
                  pl.BlockSpec((1, 1, HEAD_DIM), lambda t: ((ep(t) % nj) // tiles_per_part, 0, 0)),
                  tok_tab, tok_tab, tok_tab],
        out_specs=pl.BlockSpec((bm, bn), lambda t: (ep(t) // nj, ep(t) % nj)),
        out_shape=jax.ShapeDtypeStruct((N, width), BF16),
        scratch_shapes=[pltpu.VMEM((bm, bn), F32), pltpu.VMEM((bm, bn), F32)],
        compiler_params=_params(("arbitrary",)),
        name="in_proj_qk",
    )(h, w, qk_gain, cos_t, s1_t, s2_t)


def _cast_kernel(x_ref, o_ref):
    o_ref[...] = x_ref[0].astype(o_ref.dtype)


def _to_bf16(w3, layer, tr):
    _, R, C = w3.shape
    return pl.pallas_call(
        _cast_kernel,
        grid=(R // tr,),
        in_specs=[pl.BlockSpec((1, tr, C), lambda i: (layer, i, 0))],
        out_specs=pl.BlockSpec((tr, C), lambda i: (i, 0)),
        out_shape=jax.ShapeDtypeStruct((R, C), BF16),
        compiler_params=_params(("parallel",)),
        name="to_bf16",
    )(w3)


def _in_proj(h, w_in, qk_gain, cos_t, s1_t, s2_t, ln_g, ln_b, bm=1024, bn=1024):
    qk = _qk_proj(h, w_in, qk_gain, cos_t, s1_t, s2_t)
    v = _proj(h, w_in, 2 * D_ATTN, D_ATTN, "none", bm=bm, bn=bn)
    u = _proj(h, w_in, 3 * D_ATTN, D_GMLP, "gelu", bm=bm, bn=bn)
    row = pl.BlockSpec((1, bn), lambda i, j: (0, j))
    gv = _proj(h, w_in, 3 * D_ATTN + D_GMLP, D_GMLP, "gelu_ln", (ln_g, ln_b), (row, row), bm, bn)
    return qk, v, u, gv


def _attn_kernel(q_ref, k_ref, v_ref, bias_ref, o_ref, *, tq):
    S = q_ref.shape[1]
    n_q = S // tq
    for i in range(n_q):
        kend = (i + 1) * tq
        q = q_ref[0, i * tq:(i + 1) * tq, :]
        s = lax.dot_general(q, k_ref[0, :kend, :], (((1,), (1,)), ((), ())),
                            preferred_element_type=F32)
        off = (n_q - 1 - i) * tq
        s = s + bias_ref[:, off:off + kend]
        m = jnp.max(s, axis=-1, keepdims=True)
        p = jnp.exp(s - m)
        l = jnp.sum(p, axis=-1, keepdims=True)
        o = jnp.dot(p.astype(BF16), v_ref[0, :kend, :], preferred_element_type=F32)
        o_ref[0, i * tq:(i + 1) * tq, :] = (o / l).astype(o_ref.dtype)


def _attn_bias_table(S, tq):
    r = jnp.arange(tq, dtype=I32)[:, None]
    u = jnp.arange(S, dtype=I32)[None, :]
    d = r + (S - tq) - u
    mult = jnp.zeros((tq, S), F32)
    for window, dil in DILATED_PATTERNS:
        mult = mult + ((d >= 0) & (d % dil == 0) & (d <= window)).astype(F32)
    return jnp.where(mult > 0, jnp.log(jnp.maximum(mult, 1.0)), MASKED)


def _attention(qk3, v3, tq=256):
    B, S, _ = v3.shape
    bias = _attn_bias_table(S, tq)
    H = N_HEADS_ATTN
    return pl.pallas_call(
        functools.partial(_attn_kernel, tq=tq),
        grid=(B, H),
        in_specs=[pl.BlockSpec((1, S, HEAD_DIM), lambda b, h: (b, 0, h)),
                  pl.BlockSpec((1, S, HEAD_DIM), lambda b, h: (b, 0, H + h)),
                  pl.BlockSpec((1, S, HEAD_DIM), lambda b, h: (b, 0, h)),
                  pl.BlockSpec((tq, S), lambda b, h: (0, 0))],
        out_specs=pl.BlockSpec((1, S, HEAD_DIM), lambda b, h: (b, 0, h)),
        out_shape=jax.ShapeDtypeStruct((B, S, D_ATTN), BF16),
        compiler_params=_params(("parallel", "parallel")),
        name="dilated_attn",
    )(qk3, qk3, v3, bias)


def _gmlp_kernel(a_ref, u_ref, gv_ref, ws_ref, bst_ref, on_ref, o_ref, g_scr):
    tg = a_ref.shape[0]
    row = lax.broadcasted_iota(I32, (GMLP_CHUNK, GMLP_CHUNK), 0)
    col = lax.broadcasted_iota(I32, (GMLP_CHUNK, GMLP_CHUNK), 1)
    causal = row >= col
    for g in range(N_GROUPS_GMLP):
        w = jnp.where(causal, ws_ref[g], 0.0).astype(BF16)
        bias = bst_ref[:, g:g + 1]
        sl = slice(g * HEAD_DIM, (g + 1) * HEAD_DIM)
        for c in range(tg // GMLP_CHUNK):
            rows = slice(c * GMLP_CHUNK, (c + 1) * GMLP_CHUNK)
            mixed = jnp.dot(w, gv_ref[rows, sl], preferred_element_type=F32) + bias
            g_scr[rows, sl] = u_ref[rows, sl].astype(F32) * mixed
    gm = g_scr[...]
    gm = gm * lax.rsqrt(jnp.mean(gm * gm, axis=-1, keepdims=True) + EPS)
    o_ref[:, D_ATTN:] = (gm * on_ref[:, D_ATTN:]).astype(o_ref.dtype)
    a = a_ref[...].astype(F32)
    a = a * lax.rsqrt(jnp.mean(a * a, axis=-1, keepdims=True) + EPS)
    o_ref[:, :D_ATTN] = (a * on_ref[:, :D_ATTN]).astype(o_ref.dtype)


def _gmlp_and_norms(attn2, u, gv, w_s, b_s_t, out_norm, tg=512):
    N = attn2.shape[0]
    D_mix = D_ATTN + D_GMLP
    return pl.pallas_call(
        _gmlp_kernel,
        grid=(N // tg,),
        in_specs=[pl.BlockSpec((tg, D_ATTN), lambda i: (i, 0)),
                  pl.BlockSpec((tg, D_GMLP), lambda i: (i, 0)),
                  pl.BlockSpec((tg, D_GMLP), lambda i: (i, 0)),
                  pl.BlockSpec((N_GROUPS_GMLP, GMLP_CHUNK, GMLP_CHUNK), lambda i: (0, 0, 0)),
                  pl.BlockSpec((GMLP_CHUNK, N_GROUPS_GMLP), lambda i: (0, 0)),
                  pl.BlockSpec((1, D_mix), lambda i: (0, 0))],
        out_specs=pl.BlockSpec((tg, D_mix), lambda i: (i, 0)),
        out_shape=jax.ShapeDtypeStruct((N, D_mix), BF16),
        scratch_shapes=[pltpu.VMEM((tg, D_GMLP), F32)],
        compiler_params=_params(("parallel",)),
        name="gmlp_norms",
    )(attn2, u, gv, w_s, b_s_t, out_norm)


def _outproj_kernel(m_ref, w_ref, x_ref, mb_ref, ml_ref, o_ref):
    acc = jnp.dot(m_ref[...], w_ref[...], preferred_element_type=F32)
    o_ref[...] = x_ref[...] + _mod_row(mb_ref, ml_ref, 2) * acc


def _out_proj(mix, w_out, x2, mod_base, mod_l, S, bm=1024, bn=512):
    N, D_mix = mix.shape
    D = w_out.shape[1]
    per_b = S // bm
    return pl.pallas_call(
        _outproj_kernel,
        grid=(N // bm, D // bn),
        in_specs=[pl.BlockSpec((bm, D_mix), lambda i, j: (i, 0)),
                  pl.BlockSpec((D_mix, bn), lambda i, j: (0, j)),
                  pl.BlockSpec((bm, bn), lambda i, j: (i, j)),
                  pl.BlockSpec((1, N_MOD, bn), lambda i, j: (i // per_b, 0, j)),
                  pl.BlockSpec((N_MOD, bn), lambda i, j: (0, j))],
        out_specs=pl.BlockSpec((bm, bn), lambda i, j: (i, j)),
        out_shape=jax.ShapeDtypeStruct((N, D), F32),
        compiler_params=_params(("parallel", "arbitrary")),
        name="out_proj",
    )(mix, w_out, x2, mod_base, mod_l)


def _router_kernel(x_ref, mb_ref, ml_ref, wr_ref, br_ref,
                   h_ref, idx_ref, gate_ref, rank_ref, cnt_ref, cnt_scr):
    i = pl.program_id(0)
    tm = x_ref.shape[0]

    @pl.when(i == 0)
    def _():
        cnt_scr[...] = jnp.zeros_like(cnt_scr)

    h = _norm_mod(x_ref[...], mb_ref, ml_ref, 3, 4)
    h_ref[...] = _pack_halves(h)
    logits = jnp.dot(h.astype(BF16), wr_ref[...].astype(BF16),
                     preferred_element_type=F32) + br_ref[...]
    lane = lax.broadcasted_iota(I32, (tm, LANES), 1)
    lane_f = lane.astype(F32)
    vals = jnp.where(lane < N_EXPERTS, logits, -jnp.inf)
    tops, sels = [], []
    idx_out = jnp.zeros((tm, LANES), F32)
    for k in range(TOP_K):
        m = jnp.max(vals, axis=-1, keepdims=True)
        first = jnp.min(jnp.where(vals == m, lane_f, float(LANES)), axis=-1, keepdims=True)
        sel = lane_f == first
        tops.append(m)
        sels.append(sel)
        idx_out = jnp.where(lane == k, first, idx_out)
        vals = jnp.where(sel, -jnp.inf, vals)
    exps = [jnp.exp(t - tops[0]) for t in tops]
    denom = exps[0] + exps[1] + exps[2] + exps[3]
    gate_out = jnp.zeros((tm, LANES), F32)
    for k in range(TOP_K):
        gate_out = jnp.where(lane == k, exps[k] / denom, gate_out)

    chosen = jnp.zeros((tm, LANES), F32)
    for sel in sels:
        chosen = chosen + sel.astype(F32)
    r = lax.broadcasted_iota(I32, (tm, tm), 0)
    c = lax.broadcasted_iota(I32, (tm, tm), 1)
    earlier = jnp.where(r > c, 1.0, 0.0).astype(BF16)
    before = jnp.dot(earlier, chosen.astype(BF16), preferred_element_type=F32) + cnt_scr[...]
    rank_out = jnp.zeros((tm, LANES), F32)
    for k in range(TOP_K):
        rk = jnp.sum(jnp.where(sels[k], before, 0.0), axis=-1, keepdims=True)
        rank_out = jnp.where(lane == k, rk, rank_out)
    cnt_scr[...] = cnt_scr[...] + jnp.sum(chosen, axis=0, keepdims=True)

    idx_ref[...] = idx_out.astype(I32)
    gate_ref[...] = gate_out
    rank_ref[...] = rank_out.astype(I32)
    cnt_ref[...] = cnt_scr[...].astype(I32)


def _ffn_norm_router(x2, mod_base, mod_l, w_router_p, b_router_p, S, tm=256):
    N, D = x2.shape
    per_b = S // tm
    tok_lane = pl.BlockSpec((tm, LANES), lambda i: (i, 0))
    return pl.pallas_call(
        _router_kernel,
        grid=(N // tm,),
        in_specs=[pl.BlockSpec((tm, D), lambda i: (i, 0)),
                  pl.BlockSpec((1, N_MOD, D), lambda i: (i // per_b, 0, 0)),
                  pl.BlockSpec((N_MOD, D), lambda i: (0, 0)),
                  pl.BlockSpec((D, LANES), lambda i: (0, 0)),
                  pl.BlockSpec((1, LANES), lambda i: (0, 0))],
        out_specs=[pl.BlockSpec((tm, D // 2), lambda i: (i, 0)), tok_lane, tok_lane, tok_lane,
                   pl.BlockSpec((1, LANES), lambda i: (0, 0))],
        out_shape=[jax.ShapeDtypeStruct((N, D // 2), jnp.uint32),
                   jax.ShapeDtypeStruct((N, LANES), I32),
                   jax.ShapeDtypeStruct((N, LANES), F32),
                   jax.ShapeDtypeStruct((N, LANES), I32),
                   jax.ShapeDtypeStruct((1, LANES), I32)],
        scratch_shapes=[pltpu.VMEM((1, LANES), F32)],
        compiler_params=_params(("arbitrary",)),
        name="ffn_norm_router",
    )(x2, mod_base, mod_l, w_router_p, b_router_p)


MXU_DIM = 256


def _split_gate_up_kernel(w_ref, g_ref, u_ref):
    r = lax.broadcasted_iota(I32, (MXU_DIM, MXU_DIM), 0)
    c = lax.broadcasted_iota(I32, (MXU_DIM, MXU_DIM), 1)
    src = jnp.where(c < LANES, 2 * c, 2 * (c - LANES) + 1)
    perm = jnp.where(r == src, 1.0, 0.0).astype(BF16)
    for ch in range(w_ref.shape[3] // MXU_DIM):
        w = w_ref[0, 0, :, ch * MXU_DIM:(ch + 1) * MXU_DIM].astype(BF16)
        res = jnp.dot(w, perm, preferred_element_type=F32)
        g_ref[0, :, ch * LANES:(ch + 1) * LANES] = res[:, :LANES].astype(BF16)
        u_ref[0, :, ch * LANES:(ch + 1) * LANES] = res[:, LANES:].astype(BF16)


def _split_gate_up(w_gate_up, layer, tr=1024):
    _, E, D, two_ff = w_gate_up.shape
    d_ff = two_ff // 2
    out = jax.ShapeDtypeStruct((E, D, d_ff), BF16)
    return pl.pallas_call(
        _split_gate_up_kernel,
        grid=(E, D // tr),
        in_specs=[pl.BlockSpec((1, 1, tr, two_ff), lambda e, i: (layer, e, i, 0))],
        out_specs=[pl.BlockSpec((1, tr, d_ff), lambda e, i: (e, i, 0)),
                   pl.BlockSpec((1, tr, d_ff), lambda e, i: (e, i, 0))],
        out_shape=[out, out],
        compiler_params=_params(("parallel", "parallel")),
        name="split_gate_up",
    )(w_gate_up)


def _expert_kernel(be_ref, nused_ref, stok_ref, h_hbm, wg_ref, wu_ref, wd_ref,
                   bg_ref, bu_ref, bd_ref, y_ref, xbuf0, xbuf1, sem, *, tm):
    b = pl.program_id(0)
    nused = nused_ref[0]
    bufs = (xbuf0, xbuf1)

    def row_copy(tok, slot, r):
        return pltpu.make_async_copy(h_hbm.at[pl.ds(tok, 1), :],
                                     bufs[slot].at[pl.ds(r, 1), :], sem.at[slot])

    def wait_block(slot):
        pltpu.make_async_copy(h_hbm.at[pl.ds(0, tm), :], bufs[slot], sem.at[slot]).wait()

    @pl.when(b == 0)
    def _():
        def body(r, carry):
            row_copy(stok_ref[r], 0, r).start()
            return carry
        lax.fori_loop(0, tm, body, 0, unroll=8)

    def compute(slot):
        wait_block(slot)
        for r in range(tm):
            row_copy(stok_ref[(b + 1) * tm + r], 1 - slot, r).start()
        lo, hi = _unpack_halves(bufs[slot][...])
        x = jnp.concatenate([lo.astype(BF16), hi.astype(BF16)], axis=1)
        gate = jnp.dot(x, wg_ref[0], preferred_element_type=F32) + bg_ref[0]
        up = jnp.dot(x, wu_ref[0], preferred_element_type=F32) + bu_ref[0]
        gate = jnp.minimum(gate, SWIGLU_LIMIT)
        up = jnp.clip(up, -SWIGLU_LIMIT, SWIGLU_LIMIT)
        act = (up + 1.0) * gate * jax.nn.sigmoid(SWIGLU_ALPHA * gate)
        y = jnp.dot(act.astype(BF16), wd_ref[0], preferred_element_type=F32) + bd_ref[0]
        y_ref[...] = _pack_halves(y)

    for slot in range(2):
        @pl.when((b < nused) & (b % 2 == slot))
        def _(slot=slot):
            compute(slot)

        @pl.when((b == nused) & (b % 2 == slot))
        def _(slot=slot):
            wait_block(slot)

    @pl.when(b >= nused)
    def _():
        y_ref[...] = jnp.zeros_like(y_ref)


def _experts(h2, block_e, nused, slot_tok, wg, wu, wd, bg, bu, bd, tm):
    N, half = h2.shape
    D = 2 * half
    n_blocks = block_e.shape[0]
    d_ff = wg.shape[2]
    e_map = lambda b, be, nu, st: (be[b], 0, 0)
    grid_spec = pltpu.PrefetchScalarGridSpec(
        num_scalar_prefetch=3,
        grid=(n_blocks,),
        in_specs=[pl.BlockSpec(memory_space=pl.ANY),
                  pl.BlockSpec((1, D, d_ff), e_map),
                  pl.BlockSpec((1, D, d_ff), e_map),
                  pl.BlockSpec((1, d_ff, D), e_map),
                  pl.BlockSpec((1, 1, d_ff), e_map),
                  pl.BlockSpec((1, 1, d_ff), e_map),
                  pl.BlockSpec((1, 1, D), e_map)],
        out_specs=pl.BlockSpec((tm, half), lambda b, be, nu, st: (b, 0)),
        scratch_shapes=[pltpu.VMEM((tm, half), jnp.uint32), pltpu.VMEM((tm, half), jnp.uint32),
                        pltpu.SemaphoreType.DMA((2,))],
    )
    return pl.pallas_call(
        functools.partial(_expert_kernel, tm=tm),
        grid_spec=grid_spec,
        out_shape=jax.ShapeDtypeStruct((n_blocks * tm, half), jnp.uint32),
        compiler_params=_params(("arbitrary",)),
        name="experts",
    )(block_e, nused, slot_tok, h2, wg, wu, wd, bg, bu, bd)


SUBLANES = 8


def _combine_kernel(dest_ref, y_hbm, x_ref, gate_ref, mb_ref, ml_ref, o_ref, buf0, buf1, sem, *, tc):
    i = pl.program_id(0)
    n = pl.num_programs(0)
    bufs = (buf0, buf1)
    half = buf0.shape[2]

    def row_copy(d, slot, k, r):
        return pltpu.make_async_copy(y_hbm.at[pl.ds(d, 1), :],
                                     bufs[slot].at[k, pl.ds(r, 1), :], sem.at[slot])

    def issue_rows(step, slot, r0, count):
        for rr in range(count):
            base = (step * tc + r0 + rr) * TOP_K
            for k in range(TOP_K):
                row_copy(dest_ref[base + k], slot, k, r0 + rr).start()

    def wait_step(slot):
        for k in range(TOP_K):
            pltpu.make_async_copy(y_hbm.at[pl.ds(0, tc), :], bufs[slot].at[k], sem.at[slot]).wait()

    @pl.when(i == 0)
    def _():
        def body(c, carry):
            issue_rows(0, 0, c * SUBLANES, SUBLANES)
            return carry
        lax.fori_loop(0, tc // SUBLANES, body, 0)

    nxt = jnp.minimum(i + 1, n - 1)
    gate_f = _mod_row(mb_ref, ml_ref, 5)

    def step(slot):
        wait_step(slot)

        def body(c, carry):
            r0 = pl.multiple_of(c * SUBLANES, SUBLANES)
            issue_rows(nxt, 1 - slot, r0, SUBLANES)
            rows = pl.ds(r0, SUBLANES)
            acc_lo = jnp.zeros((SUBLANES, half), F32)
            acc_hi = jnp.zeros((SUBLANES, half), F32)
            for k in range(TOP_K):
                lo, hi = _unpack_halves(bufs[slot][k, rows, :])
                g = gate_ref[rows, k:k + 1]
                acc_lo = acc_lo + g * lo
                acc_hi = acc_hi + g * hi
            o_ref[rows, :half] = x_ref[rows, :half] + gate_f[:, :half] * acc_lo
            o_ref[rows, half:] = x_ref[rows, half:] + gate_f[:, half:] * acc_hi
            return carry
        lax.fori_loop(0, tc // SUBLANES, body, 0)

        @pl.when(i == n - 1)
        def _():
            wait_step(1 - slot)

    for slot in range(2):
        @pl.when(i % 2 == slot)
        def _(slot=slot):
            step(slot)


def _combine(dest_flat, y, x2, gates, mod_base, mod_l, S, tc=256):
    N, D = x2.shape
    per_b = S // tc
    grid_spec = pltpu.PrefetchScalarGridSpec(
        num_scalar_prefetch=1,
        grid=(N // tc,),
        in_specs=[pl.BlockSpec(memory_space=pl.ANY),
                  pl.BlockSpec((tc, D), lambda i, d: (i, 0)),
                  pl.BlockSpec((tc, LANES), lambda i, d: (i, 0)),
                  pl.BlockSpec((1, N_MOD, D), lambda i, d: (i // per_b, 0, 0)),
                  pl.BlockSpec((N_MOD, D), lambda i, d: (0, 0))],
        out_specs=pl.BlockSpec((tc, D), lambda i, d: (i, 0)),
        scratch_shapes=[pltpu.VMEM((TOP_K, tc, D // 2), jnp.uint32),
                        pltpu.VMEM((TOP_K, tc, D // 2), jnp.uint32), pltpu.SemaphoreType.DMA((2,))],
    )
    return pl.pallas_call(
        functools.partial(_combine_kernel, tc=tc),
        grid_spec=grid_spec,
        out_shape=jax.ShapeDtypeStruct((N, D), F32),
        compiler_params=_params(("arbitrary",)),
        name="moe_combine",
    )(dest_flat, y, x2, gates, mod_base, mod_l)


EXPERT_ROWS = 256


def _rope_tables(positions):
    half = ROT_DIM // 2
    inv_freq = ROPE_THETA ** (-jnp.arange(half, dtype=F32) * 2.0 / ROT_DIM)
    ang = positions.astype(F32).reshape(-1, 1) * inv_freq
    cos, sin = jnp.cos(ang), jnp.sin(ang)
    n = ang.shape[0]
    ones = jnp.ones((n, HEAD_DIM - ROT_DIM), F32)
    zeros = jnp.zeros((n, HEAD_DIM - half), F32)
    cos_t = jnp.concatenate([cos, cos, ones], axis=1)
    s1_t = jnp.concatenate([-sin, zeros], axis=1)
    s2_t = jnp.concatenate([jnp.zeros((n, half), F32), sin,
                            jnp.zeros((n, HEAD_DIM - ROT_DIM), F32)], axis=1)
    return cos_t, s1_t, s2_t


def _routing_tables(top_idx, rank, counts, tm, n_blocks):
    N = top_idx.shape[0]
    nblk = (counts + tm - 1) // tm
    blk_end = jnp.cumsum(nblk)
    blk_start = blk_end - nblk
    dest = blk_start[top_idx] * tm + rank
    owner = jnp.sum((blk_end[None, :] <= jnp.arange(n_blocks, dtype=I32)[:, None]).astype(I32), axis=1)
    block_e = jnp.minimum(owner, N_EXPERTS - 1).astype(I32)
    nused = blk_end[-1:].astype(I32)
    tok = jnp.broadcast_to(jnp.arange(N, dtype=I32)[:, None], (N, TOP_K))
    slot_tok = jnp.zeros((n_blocks * tm,), I32).at[dest.reshape(-1)].set(
        tok.reshape(-1), unique_indices=True)
    return dest.reshape(-1).astype(I32), block_e, nused, slot_tok


def kernel(x, c, positions, w_mod, b_mod, mod_layer, w_in, q_norm, k_norm, w_s, b_s, v_ln_g, v_ln_b,
           out_norm, w_out, w_router, b_router, w_gate_up, b_gate_up, w_down, b_down):
    B, S, D = x.shape
    N = B * S
    depth = w_in.shape[0]
    d_ff = w_down.shape[2]
    tm_e = EXPERT_ROWS
    n_blocks = N * TOP_K // tm_e + N_EXPERTS

    mod_base = _mod_base(c, w_mod, b_mod).reshape(B, N_MOD, D)
    cos_t, s1_t, s2_t = _rope_tables(positions)
    x2 = x.reshape(N, D)

    for l in range(depth):
        mod_l = mod_layer[l]
        h = _mixer_norm(x2, mod_base, mod_l, S)
        qk_gain = jnp.stack([q_norm[l] * HEAD_DIM ** -0.5, k_norm[l]]).reshape(2, 1, HEAD_DIM)
        qk, v, u, gv = _in_proj(h, _to_bf16(w_in, l, 256), qk_gain, cos_t, s1_t, s2_t,
                                v_ln_g[l].reshape(1, D_GMLP), v_ln_b[l].reshape(1, D_GMLP))
        attn = _attention(qk.reshape(B, S, -1), v.reshape(B, S, -1))
        mix = _gmlp_and_norms(attn.reshape(N, D_ATTN), u, gv, w_s[l], jnp.transpose(b_s[l]),
                              out_norm[l].reshape(1, -1))
        x2 = _out_proj(mix, _to_bf16(w_out, l, 512), x2, mod_base, mod_l, S)

        w_router_p = jnp.pad(w_router[l], ((0, 0), (0, LANES - N_EXPERTS)))
        b_router_p = jnp.pad(b_router[l], (0, LANES - N_EXPERTS)).reshape(1, LANES)
        h2, top_idx, gates, rank, counts = _ffn_norm_router(x2, mod_base, mod_l, w_router_p,
                                                            b_router_p, S)
        dest, block_e, nused, slot_tok = _routing_tables(
            top_idx[:, :TOP_K], rank[:, :TOP_K], counts[0, :N_EXPERTS], tm_e, n_blocks)
        wg, wu = _split_gate_up(w_gate_up, l)
        bg = b_gate_up[l][:, 0::2].reshape(N_EXPERTS, 1, d_ff)
        bu = b_gate_up[l][:, 1::2].reshape(N_EXPERTS, 1, d_ff)
        wd = _to_bf16(w_down.reshape(depth, N_EXPERTS * d_ff, D), l, 512).reshape(N_EXPERTS, d_ff, D)
        y = _experts(h2, block_e, nused, slot_tok, wg, wu, wd, bg, bu,
                     b_down[l].reshape(N_EXPERTS, 1, D), tm_e)
        x2 = _combine(dest, y, x2, gates, mod_base, mod_l, S)
    return x2.reshape(B, S, D)
```

```python
import functools
import math

import jax
import jax.numpy as jnp
from jax import lax
from jax.experimental import pallas as pl
from jax.experimental.pallas import tpu as pltpu

F32 = jnp.float32
BF16 = jnp.bfloat16
I32 = jnp.int32

HEAD_DIM = 128
N_HEADS_ATTN = 16
N_GROUPS_GMLP = 16
D_ATTN = N_HEADS_ATTN * HEAD_DIM
D_GMLP = N_GROUPS_GMLP * HEAD_DIM
ROT_DIM = HEAD_DIM // 4
ROPE_THETA = 500000.0
DILATED_PATTERNS = ((128, 1), (512, 4), (2048, 16))
GMLP_CHUNK = 128
N_EXPERTS = 32
TOP_K = 4
SWIGLU_LIMIT = 7.0
SWIGLU_ALPHA = 1.702
N_MOD = 6
EPS = 1e-6
LANES = 128
MASKED = -1e30

VMEM_LIMIT = 56 * 1024 * 1024


def _params(sem, vmem=VMEM_LIMIT):
    return pltpu.CompilerParams(dimension_semantics=sem, vmem_limit_bytes=vmem)


def _mod_kernel(c_ref, w_ref, b_ref, o_ref):
    c = c_ref[...]
    s = c * jax.nn.sigmoid(c)
    o_ref[...] = jnp.dot(s.astype(BF16), w_ref[...].astype(BF16),
                         preferred_element_type=F32) + b_ref[...]


def _mod_base(c, w_mod, b_mod, tn=512):
    B, D = c.shape
    n_out = w_mod.shape[1]
    return pl.pallas_call(
        _mod_kernel,
        grid=(n_out // tn,),
        in_specs=[pl.BlockSpec((B, D), lambda j: (0, 0)),
                  pl.BlockSpec((D, tn), lambda j: (0, j)),
                  pl.BlockSpec((1, tn), lambda j: (0, j))],
        out_specs=pl.BlockSpec((B, tn), lambda j: (0, j)),
        out_shape=jax.ShapeDtypeStruct((B, n_out), F32),
        compiler_params=_params(("parallel",)),
        name="mod_base",
    )(c, w_mod, b_mod.reshape(1, n_out))


def _pack_halves(x):
    c = x.shape[1] // 2
    bits = pltpu.bitcast(x.astype(BF16).astype(F32), jnp.uint32)
    return (bits[:, c:] & jnp.uint32(0xFFFF0000)) | (bits[:, :c] >> 16)


def _unpack_halves(p):
    lo = pltpu.bitcast(p << 16, F32)
    hi = pltpu.bitcast(p & jnp.uint32(0xFFFF0000), F32)
    return lo, hi


def _mod_row(mb_ref, ml_ref, idx):
    return mb_ref[0, idx:idx + 1, :] + ml_ref[idx:idx + 1, :]


def _norm_mod(x, mb_ref, ml_ref, shift_idx, scale_idx):
    y = x * lax.rsqrt(jnp.mean(x * x, axis=-1, keepdims=True) + EPS)
    return y * (1.0 + _mod_row(mb_ref, ml_ref, scale_idx)) + _mod_row(mb_ref, ml_ref, shift_idx)


def _norm_kernel(x_ref, mb_ref, ml_ref, h_ref):
    h_ref[...] = _norm_mod(x_ref[...], mb_ref, ml_ref, 0, 1).astype(h_ref.dtype)


def _mixer_norm(x2, mod_base, mod_l, S, tm=256):
    N, D = x2.shape
    per_b = S // tm
    return pl.pallas_call(
        _norm_kernel,
        grid=(N // tm,),
        in_specs=[pl.BlockSpec((tm, D), lambda i: (i, 0)),
                  pl.BlockSpec((1, N_MOD, D), lambda i: (i // per_b, 0, 0)),
                  pl.BlockSpec((N_MOD, D), lambda i: (0, 0))],
        out_specs=pl.BlockSpec((tm, D), lambda i: (i, 0)),
        out_shape=jax.ShapeDtypeStruct((N, D), BF16),
        compiler_params=_params(("parallel",)),
        name="mixer_norm",
    )(x2, mod_base, mod_l)


def _gelu(x):
    return 0.5 * x * (1.0 + lax.erf(x * (1.0 / math.sqrt(2.0))))


def _proj_kernel(*refs, mode, sub):
    if mode == "gelu_ln":
        h_ref, w_ref, lg_ref, lb_ref, o_ref = refs
    else:
        h_ref, w_ref, o_ref = refs
    bn = o_ref.shape[1]
    for s in range(bn // sub):
        acc = jnp.dot(h_ref[...], w_ref[:, s * sub:(s + 1) * sub], preferred_element_type=F32)
        for hh in range(sub // HEAD_DIM):
            t = acc[:, hh * HEAD_DIM:(hh + 1) * HEAD_DIM]
            sl = slice(s * sub + hh * HEAD_DIM, s * sub + (hh + 1) * HEAD_DIM)
            if mode == "gelu":
                r = _gelu(t)
            elif mode == "gelu_ln":
                g = _gelu(t)
                gc = g - jnp.mean(g, axis=-1, keepdims=True)
                r = gc * lax.rsqrt(jnp.mean(gc * gc, axis=-1, keepdims=True) + EPS)
                r = r * lg_ref[:, sl] + lb_ref[:, sl]
            else:
                r = t
            o_ref[:, sl] = r.astype(o_ref.dtype)


def _proj(h, w, col0, width, mode, extra=(), extra_specs=(), bm=1024, bn=1024, sub=256):
    N, D = h.shape
    j0 = col0 // bn
    return pl.pallas_call(
        functools.partial(_proj_kernel, mode=mode, sub=sub),
        grid=(N // bm, width // bn),
        in_specs=[pl.BlockSpec((bm, D), lambda i, j: (i, 0)),
                  pl.BlockSpec((D, bn), lambda i, j: (0, j0 + j))] + list(extra_specs),
        out_specs=pl.BlockSpec((bm, bn), lambda i, j: (i, j)),
        out_shape=jax.ShapeDtypeStruct((N, width), BF16),
        compiler_params=_params(("parallel", "arbitrary")),
        name="in_proj_" + mode,
    )(h, w, *extra)


def _qk_proj_kernel(h_ref, w_ref, gain_ref, cos_ref, s1_ref, s2_ref, o_ref, acc0, acc1):
    t = pl.program_id(0)
    bn = o_ref.shape[1]

    @pl.when(t == 0)
    def _():
        acc1[...] = jnp.zeros_like(acc1)

    def step(acc_mm, acc_ep):
        acc_mm[...] = jnp.dot(h_ref[...], w_ref[...], preferred_element_type=F32)
        for hh in range(bn // HEAD_DIM):
            sl = slice(hh * HEAD_DIM, (hh + 1) * HEAD_DIM)
            a = acc_ep[:, sl]
            y = a * lax.rsqrt(jnp.mean(a * a, axis=-1, keepdims=True) + EPS) * gain_ref[0]
            r = (y * cos_ref[...] + pltpu.roll(y, HEAD_DIM - ROT_DIM // 2, 1) * s1_ref[...]
                 + pltpu.roll(y, ROT_DIM // 2, 1) * s2_ref[...])
            o_ref[:, sl] = r.astype(o_ref.dtype)

    @pl.when(t % 2 == 0)
    def _():
        step(acc0, acc1)

    @pl.when(t % 2 == 1)
    def _():
        step(acc1, acc0)


def _qk_proj(h, w, qk_gain, cos_t, s1_t, s2_t, bm=1024, bn=512):
    N, D = h.shape
    width = 2 * D_ATTN
    nj = width // bn
    n_tiles = (N // bm) * nj
    tiles_per_part = D_ATTN // bn
    mm = lambda t: jnp.minimum(t, n_tiles - 1)
    ep = lambda t: jnp.maximum(t - 1, 0)
    tok_tab = pl.BlockSpec((bm, HEAD_DIM), lambda t: (ep(t) // nj, 0))
    return pl.pallas_call(
        _qk_proj_kernel,
        grid=(n_tiles + 1,),
        in_specs=[pl.BlockSpec((bm, D), lambda t: (mm(t) // nj, 0)),
                  pl.BlockSpec((D, bn), lambda t: (0, mm(t) % nj)),
                  pl.BlockSpec((1, 1, HEAD_DIM), lambda t: ((ep(t) % nj) // tiles_per_part, 0, 0)),
                  tok_tab, tok_tab, tok_tab],
        out_specs=pl.BlockSpec((bm, bn), lambda t: (ep(t) // nj, ep(t) % nj)),
        out_shape=jax.ShapeDtypeStruct((N, width), BF16),
        scratch_shapes=[pltpu.VMEM((bm, bn), F32), pltpu.VMEM((bm, bn), F32)],
        compiler_params=_params(("arbitrary",)),
        name="in_proj_qk",
    )(h, w, qk_gain, cos_t, s1_t, s2_t)


def _cast_kernel(x_ref, o_ref):
    o_ref[...] = x_ref[0].astype(o_ref.dtype)


def _to_bf16(w3, layer, tr):
    _, R, C = w3.shape
    return pl.pallas_call(
        _cast_kernel,
        grid=(R // tr,),
        in_specs=[pl.BlockSpec((1, tr, C), lambda i: (layer, i, 0))],
        out_specs=pl.BlockSpec((tr, C), lambda i: (i, 0)),
        out_shape=jax.ShapeDtypeStruct((R, C), BF16),
        compiler_params=_params(("parallel",)),
        name="to_bf16",
    )(w3)


def _in_proj(h, w_in, qk_gain, cos_t, s1_t, s2_t, ln_g, ln_b, bm=1024, bn=1024):
    qk = _qk_proj(h, w_in, qk_gain, cos_t, s1_t, s2_t)
    v = _proj(h, w_in, 2 * D_ATTN, D_ATTN, "none", bm=bm, bn=bn)
    u = _proj(h, w_in, 3 * D_ATTN, D_GMLP, "gelu", bm=bm, bn=bn)
    row = pl.BlockSpec((1, bn), lambda i, j: (0, j))
    gv = _proj(h, w_in, 3 * D_ATTN + D_GMLP, D_GMLP, "gelu_ln", (ln_g, ln_b), (row, row), bm, bn)
    return qk, v, u, gv


def _attn_kernel(q_ref, k_ref, v_ref, bias_ref, o_ref, *, tq):
    S = q_ref.shape[1]
    n_q = S // tq
    for i in range(n_q):
        kend = (i + 1) * tq
        q = q_ref[0, i * tq:(i + 1) * tq, :]
        s = lax.dot_general(q, k_ref[0, :kend, :], (((1,), (1,)), ((), ())),
                            preferred_element_type=F32)
        off = (n_q - 1 - i) * tq
        s = s + bias_ref[:, off:off + kend]
        m = jnp.max(s, axis=-1, keepdims=True)
        p = jnp.exp(s - m)
        l = jnp.sum(p, axis=-1, keepdims=True)
        o = jnp.dot(p.astype(BF16), v_ref[0, :kend, :], preferred_element_type=F32)
        o_ref[0, i * tq:(i + 1) * tq, :] = (o / l).astype(o_ref.dtype)


def _attn_bias_table(S, tq):
    r = jnp.arange(tq, dtype=I32)[:, None]
    u = jnp.arange(S, dtype=I32)[None, :]
    d = r + (S - tq) - u
    mult = jnp.zeros((tq, S), F32)
    for window, dil in DILATED_PATTERNS:
        mult = mult + ((d >= 0) & (d % dil == 0) & (d <= window)).astype(F32)
    return jnp.where(mult > 0, jnp.log(jnp.maximum(mult, 1.0)), MASKED)


def _attention(qk3, v3, tq=256):
    B, S, _ = v3.shape
    bias = _attn_bias_table(S, tq)
    H = N_HEADS_ATTN
    return pl.pallas_call(
        functools.partial(_attn_kernel, tq=tq),
        grid=(B, H),
        in_specs=[pl.BlockSpec((1, S, HEAD_DIM), lambda b, h: (b, 0, h)),
                  pl.BlockSpec((1, S, HEAD_DIM), lambda b, h: (b, 0, H + h)),
                  pl.BlockSpec((1, S, HEAD_DIM), lambda b, h: (b, 0, h)),
                  pl.BlockSpec((tq, S), lambda b, h: (0, 0))],
        out_specs=pl.BlockSpec((1, S, HEAD_DIM), lambda b, h: (b, 0, h)),
        out_shape=jax.ShapeDtypeStruct((B, S, D_ATTN), BF16),
        compiler_params=_params(("parallel", "parallel")),
        name="dilated_attn",
    )(qk3, qk3, v3, bias)


def _gmlp_kernel(a_ref, u_ref, gv_ref, ws_ref, bst_ref, on_ref, o_ref, g_scr):
    tg = a_ref.shape[0]
    row = lax.broadcasted_iota(I32, (GMLP_CHUNK, GMLP_CHUNK), 0)
    col = lax.broadcasted_iota(I32, (GMLP_CHUNK, GMLP_CHUNK), 1)
    causal = row >= col
    for g in range(N_GROUPS_GMLP):
        w = jnp.where(causal, ws_ref[g], 0.0).astype(BF16)
        bias = bst_ref[:, g:g + 1]
        sl = slice(g * HEAD_DIM, (g + 1) * HEAD_DIM)
        for c in range(tg // GMLP_CHUNK):
            rows = slice(c * GMLP_CHUNK, (c + 1) * GMLP_CHUNK)
            mixed = jnp.dot(w, gv_ref[rows, sl], preferred_element_type=F32) + bias
            g_scr[rows, sl] = u_ref[rows, sl].astype(F32) * mixed
    gm = g_scr[...]
    gm = gm * lax.rsqrt(jnp.mean(gm * gm, axis=-1, keepdims=True) + EPS)
    o_ref[:, D_ATTN:] = (gm * on_ref[:, D_ATTN:]).astype(o_ref.dtype)
    a = a_ref[...].astype(F32)
    a = a * lax.rsqrt(jnp.mean(a * a, axis=-1, keepdims=True) + EPS)
    o_ref[:, :D_ATTN] = (a * on_ref[:, :D_ATTN]).astype(o_ref.dtype)


def _gmlp_and_norms(attn2, u, gv, w_s, b_s_t, out_norm, tg=512):
    N = attn2.shape[0]
    D_mix = D_ATTN + D_GMLP
    return pl.pallas_call(
        _gmlp_kernel,
        grid=(N // tg,),
        in_specs=[pl.BlockSpec((tg, D_ATTN), lambda i: (i, 0)),
                  pl.BlockSpec((tg, D_GMLP), lambda i: (i, 0)),
                  pl.BlockSpec((tg, D_GMLP), lambda i: (i, 0)),
                  pl.BlockSpec((N_GROUPS_GMLP, GMLP_CHUNK, GMLP_CHUNK), lambda i: (0, 0, 0)),
                  pl.BlockSpec((GMLP_CHUNK, N_GROUPS_GMLP), lambda i: (0, 0)),
                  pl.BlockSpec((1, D_mix), lambda i: (0, 0))],
        out_specs=pl.BlockSpec((tg, D_mix), lambda i: (i, 0)),
        out_shape=jax.ShapeDtypeStruct((N, D_mix), BF16),
        scratch_shapes=[pltpu.VMEM((tg, D_GMLP), F32)],
        compiler_params=_params(("parallel",)),
        name="gmlp_norms",
    )(attn2, u, gv, w_s, b_s_t, out_norm)


def _outproj_kernel(m_ref, w_ref, x_ref, mb_ref, ml_ref, o_ref):
    acc = jnp.dot(m_ref[...], w_ref[...], preferred_element_type=F32)
    o_ref[...] = x_ref[...] + _mod_row(mb_ref, ml_ref, 2) * acc


def _out_proj(mix, w_out, x2, mod_base, mod_l, S, bm=1024, bn=512):
    N, D_mix = mix.shape
    D = w_out.shape[1]
    per_b = S // bm
    return pl.pallas_call(
        _outproj_kernel,
        grid=(N // bm, D // bn),
        in_specs=[pl.BlockSpec((bm, D_mix), lambda i, j: (i, 0)),
                  pl.BlockSpec((D_mix, bn), lambda i, j: (0, j)),
                  pl.BlockSpec((bm, bn), lambda i, j: (i, j)),
                  pl.BlockSpec((1, N_MOD, bn), lambda i, j: (i // per_b, 0, j)),
                  pl.BlockSpec((N_MOD, bn), lambda i, j: (0, j))],
        out_specs=pl.BlockSpec((bm, bn), lambda i, j: (i, j)),
        out_shape=jax.ShapeDtypeStruct((N, D), F32),
        compiler_params=_params(("parallel", "arbitrary")),
        name="out_proj",
    )(mix, w_out, x2, mod_base, mod_l)


def _router_kernel(x_ref, mb_ref, ml_ref, wr_ref, br_ref,
                   h_ref, idx_ref, gate_ref, rank_ref, cnt_ref, cnt_scr):
    i = pl.program_id(0)
    tm = x_ref.shape[0]

    @pl.when(i == 0)
    def _():
        cnt_scr[...] = jnp.zeros_like(cnt_scr)

    h = _norm_mod(x_ref[...], mb_ref, ml_ref, 3, 4)
    h_ref[...] = _pack_halves(h)
    logits = jnp.dot(h.astype(BF16), wr_ref[...].astype(BF16),
                     preferred_element_type=F32) + br_ref[...]
    lane = lax.broadcasted_iota(I32, (tm, LANES), 1)
    lane_f = lane.astype(F32)
    vals = jnp.where(lane < N_EXPERTS, logits, -jnp.inf)
    tops, sels = [], []
    idx_out = jnp.zeros((tm, LANES), F32)
    for k in range(TOP_K):
        m = jnp.max(vals, axis=-1, keepdims=True)
        first = jnp.min(jnp.where(vals == m, lane_f, float(LANES)), axis=-1, keepdims=True)
        sel = lane_f == first
        tops.append(m)
        sels.append(sel)
        idx_out = jnp.where(lane == k, first, idx_out)
        vals = jnp.where(sel, -jnp.inf, vals)
    exps = [jnp.exp(t - tops[0]) for t in tops]
    denom = exps[0] + exps[1] + exps[2] + exps[3]
    gate_out = jnp.zeros((tm, LANES), F32)
    for k in range(TOP_K):
        gate_out = jnp.where(lane == k, exps[k] / denom, gate_out)

    chosen = jnp.zeros((tm, LANES), F32)
    for sel in sels:
        chosen = chosen + sel.astype(F32)
    r = lax.broadcasted_iota(I32, (tm, tm), 0)
    c = lax.broadcasted_iota(I32, (tm, tm), 1)
    earlier = jnp.where(r > c, 1.0, 0.0).astype(BF16)
    before = jnp.dot(earlier, chosen.astype(BF16), preferred_element_type=F32) + cnt_scr[...]
    rank_out = jnp.zeros((tm, LANES), F32)
    for k in range(TOP_K):
        rk = jnp.sum(jnp.where(sels[k], before, 0.0), axis=-1, keepdims=True)
        rank_out = jnp.where(lane == k, rk, rank_out)
    cnt_scr[...] = cnt_scr[...] + jnp.sum(chosen, axis=0, keepdims=True)

    idx_ref[...] = idx_out.astype(I32)
    gate_ref[...] = gate_out
    rank_ref[...] = rank_out.astype(I32)
    cnt_ref[...] = cnt_scr[...].astype(I32)


def _ffn_norm_router(x2, mod_base, mod_l, w_router_p, b_router_p, S, tm=256):
    N, D = x2.shape
    per_b = S // tm
    tok_lane = pl.BlockSpec((tm, LANES), lambda i: (i, 0))
    return pl.pallas_call(
        _router_kernel,
        grid=(N // tm,),
        in_specs=[pl.BlockSpec((tm, D), lambda i: (i, 0)),
                  pl.BlockSpec((1, N_MOD, D), lambda i: (i // per_b, 0, 0)),
                  pl.BlockSpec((N_MOD, D), lambda i: (0, 0)),
                  pl.BlockSpec((D, LANES), lambda i: (0, 0)),
                  pl.BlockSpec((1, LANES), lambda i: (0, 0))],
        out_specs=[pl.BlockSpec((tm, D // 2), lambda i: (i, 0)), tok_lane, tok_lane, tok_lane,
                   pl.BlockSpec((1, LANES), lambda i: (0, 0))],
        out_shape=[jax.ShapeDtypeStruct((N, D // 2), jnp.uint32),
                   jax.ShapeDtypeStruct((N, LANES), I32),
                   jax.ShapeDtypeStruct((N, LANES), F32),
                   jax.ShapeDtypeStruct((N, LANES), I32),
                   jax.ShapeDtypeStruct((1, LANES), I32)],
        scratch_shapes=[pltpu.VMEM((1, LANES), F32)],
        compiler_params=_params(("arbitrary",)),
        name="ffn_norm_router",
    )(x2, mod_base, mod_l, w_router_p, b_router_p)


MXU_DIM = 256


def _split_gate_up_kernel(w_ref, g_ref, u_ref):
    r = lax.broadcasted_iota(I32, (MXU_DIM, MXU_DIM), 0)
    c = lax.broadcasted_iota(I32, (MXU_DIM, MXU_DIM), 1)
    src = jnp.where(c < LANES, 2 * c, 2 * (c - LANES) + 1)
    perm = jnp.where(r == src, 1.0, 0.0).astype(BF16)
    for ch in range(w_ref.shape[3] // MXU_DIM):
        w = w_ref[0, 0, :, ch * MXU_DIM:(ch + 1) * MXU_DIM].astype(BF16)
        res = jnp.dot(w, perm, preferred_element_type=F32)
        g_ref[0, :, ch * LANES:(ch + 1) * LANES] = res[:, :LANES].astype(BF16)
        u_ref[0, :, ch * LANES:(ch + 1) * LANES] = res[:, LANES:].astype(BF16)


def _split_gate_up(w_gate_up, layer, tr=1024):
    _, E, D, two_ff = w_gate_up.shape
    d_ff = two_ff // 2
    out = jax.ShapeDtypeStruct((E, D, d_ff), BF16)
    return pl.pallas_call(
        _split_gate_up_kernel,
        grid=(E, D // tr),
        in_specs=[pl.BlockSpec((1, 1, tr, two_ff), lambda e, i: (layer, e, i, 0))],
        out_specs=[pl.BlockSpec((1, tr, d_ff), lambda e, i: (e, i, 0)),
                   pl.BlockSpec((1, tr, d_ff), lambda e, i: (e, i, 0))],
        out_shape=[out, out],
        compiler_params=_params(("parallel", "parallel")),
        name="split_gate_up",
    )(w_gate_up)


def _dispatch_kernel(dest_ref, pad_start_ref, pad_len_ref, nused_ref, h_hbm, zero_hbm, xs_hbm,
                     sem, pad_sem, blk_sem, *, td, tm):
    i = pl.program_id(0)
    n = pl.num_programs(0)
    n_blocks = xs_hbm.shape[0] // tm

    def body(r, carry):
        t = i * td + r
        for k in range(TOP_K):
            pltpu.make_async_copy(h_hbm.at[pl.ds(t, 1), :], xs_hbm.at[pl.ds(dest_ref[t * TOP_K + k], 1), :],
                                  sem.at[i % 2]).start()
        return carry
    lax.fori_loop(0, td, body, 0, unroll=2)

    def wait_step(parity):
        pltpu.make_async_copy(h_hbm.at[pl.ds(0, td * TOP_K), :], xs_hbm.at[pl.ds(0, td * TOP_K), :],
                              sem.at[parity]).wait()

    @pl.when(i > 0)
    def _():
        wait_step((i - 1) % 2)

    @pl.when(i == n - 1)
    def _():
        wait_step(i % 2)

        def per_expert(e, total):
            start = pad_start_ref[e]
            count = pad_len_ref[e]

            def fill(r, carry):
                pltpu.make_async_copy(zero_hbm.at[pl.ds(0, 1), :], xs_hbm.at[pl.ds(start + r, 1), :],
                                      pad_sem).start()
                return carry
            lax.fori_loop(0, count, fill, 0)
            return total + count
        total = lax.fori_loop(0, N_EXPERTS, per_expert, 0)

        def block_copy(blk):
            return pltpu.make_async_copy(zero_hbm, xs_hbm.at[pl.ds(pl.multiple_of(blk * tm, tm), tm), :],
                                         blk_sem)

        def fill_block(blk, carry):
            block_copy(blk).start()
            return carry
        lax.fori_loop(nused_ref[0], n_blocks, fill_block, 0)

        def retire(r, carry):
            pltpu.make_async_copy(zero_hbm.at[pl.ds(0, 1), :], xs_hbm.at[pl.ds(0, 1), :], pad_sem).wait()
            return carry
        lax.fori_loop(0, total, retire, 0)

        def retire_block(blk, carry):
            block_copy(blk).wait()
            return carry
        lax.fori_loop(nused_ref[0], n_blocks, retire_block, 0)


def _dispatch(dest_flat, pad_start, pad_len, nused, h2, n_slots, tm, td=512):
    N, half = h2.shape
    grid_spec = pltpu.PrefetchScalarGridSpec(
        num_scalar_prefetch=4,
        grid=(N // td,),
        in_specs=[pl.BlockSpec(memory_space=pl.ANY), pl.BlockSpec(memory_space=pl.ANY)],
        out_specs=pl.BlockSpec(memory_space=pl.ANY),
        scratch_shapes=[pltpu.SemaphoreType.DMA((2,)), pltpu.SemaphoreType.DMA(()),
                        pltpu.SemaphoreType.DMA(())],
    )
    return pl.pallas_call(
        functools.partial(_dispatch_kernel, td=td, tm=tm),
        grid_spec=grid_spec,
        out_shape=jax.ShapeDtypeStruct((n_slots, half), jnp.uint32),
        compiler_params=_params(("arbitrary",)),
        name="moe_dispatch",
    )(dest_flat, pad_start, pad_len, nused, h2, jnp.zeros((tm, half), jnp.uint32))


def _expert_kernel(be_ref, nused_ref, x_ref, wg_ref, wu_ref, wd_ref, bg_ref, bu_ref, bd_ref, y_ref):
    b = pl.program_id(0)

    @pl.when(b < nused_ref[0])
    def _():
        lo, hi = _unpack_halves(x_ref[...])
        x = jnp.concatenate([lo.astype(BF16), hi.astype(BF16)], axis=1)
        gate = jnp.dot(x, wg_ref[0], preferred_element_type=F32) + bg_ref[0]
        up = jnp.dot(x, wu_ref[0], preferred_element_type=F32) + bu_ref[0]
        gate = jnp.minimum(gate, SWIGLU_LIMIT)
        up = jnp.clip(up, -SWIGLU_LIMIT, SWIGLU_LIMIT)
        act = (up + 1.0) * gate * jax.nn.sigmoid(SWIGLU_ALPHA * gate)
        y = jnp.dot(act.astype(BF16), wd_ref[0], preferred_element_type=F32) + bd_ref[0]
        y_ref[...] = _pack_halves(y)

    @pl.when(b >= nused_ref[0])
    def _():
        y_ref[...] = jnp.zeros_like(y_ref)


def _experts(xs, block_e, nused, wg, wu, wd, bg, bu, bd, tm):
    half = xs.shape[1]
    D = 2 * half
    n_blocks = block_e.shape[0]
    d_ff = wg.shape[2]
    e_map = lambda b, be, nu: (be[b], 0, 0)
    x_map = lambda b, be, nu: (jnp.minimum(b, nu[0] - 1), 0)
    grid_spec = pltpu.PrefetchScalarGridSpec(
        num_scalar_prefetch=2,
        grid=(n_blocks,),
        in_specs=[pl.BlockSpec((tm, half), x_map),
                  pl.BlockSpec((1, D, d_ff), e_map),
                  pl.BlockSpec((1, D, d_ff), e_map),
                  pl.BlockSpec((1, d_ff, D), e_map),
                  pl.BlockSpec((1, 1, d_ff), e_map),
                  pl.BlockSpec((1, 1, d_ff), e_map),
                  pl.BlockSpec((1, 1, D), e_map)],
        out_specs=pl.BlockSpec((tm, half), lambda b, be, nu: (b, 0)),
    )
    return pl.pallas_call(
        _expert_kernel,
        grid_spec=grid_spec,
        out_shape=jax.ShapeDtypeStruct((n_blocks * tm, half), jnp.uint32),
        compiler_params=_params(("arbitrary",)),
        name="experts",
    )(block_e, nused, xs, wg, wu, wd, bg, bu, bd)


SUBLANES = 8


def _combine_kernel(dest_ref, y_hbm, x_ref, gate_ref, mb_ref, ml_ref, o_ref, buf0, buf1, sem, *, tc):
    i = pl.program_id(0)
    n = pl.num_programs(0)
    bufs = (buf0, buf1)
    half = buf0.shape[2]

    def row_copy(d, slot, k, r):
        return pltpu.make_async_copy(y_hbm.at[pl.ds(d, 1), :],
                                     bufs[slot].at[k, pl.ds(r, 1), :], sem.at[slot])

    def issue_rows(step, slot, r0, count):
        for rr in range(count):
            base = (step * tc + r0 + rr) * TOP_K
            for k in range(TOP_K):
                row_copy(dest_ref[base + k], slot, k, r0 + rr).start()

    def wait_step(slot):
        for k in range(TOP_K):
            pltpu.make_async_copy(y_hbm.at[pl.ds(0, tc), :], bufs[slot].at[k], sem.at[slot]).wait()

    @pl.when(i == 0)
    def _():
        def body(c, carry):
            issue_rows(0, 0, c * SUBLANES, SUBLANES)
            return carry
        lax.fori_loop(0, tc // SUBLANES, body, 0)

    nxt = jnp.minimum(i + 1, n - 1)
    gate_f = _mod_row(mb_ref, ml_ref, 5)

    def step(slot):
        wait_step(slot)

        def body(c, carry):
            r0 = pl.multiple_of(c * SUBLANES, SUBLANES)
            issue_rows(nxt, 1 - slot, r0, SUBLANES)
            rows = pl.ds(r0, SUBLANES)
            acc_lo = jnp.zeros((SUBLANES, half), F32)
            acc_hi = jnp.zeros((SUBLANES, half), F32)
            for k in range(TOP_K):
                lo, hi = _unpack_halves(bufs[slot][k, rows, :])
                g = gate_ref[rows, k:k + 1]
                acc_lo = acc_lo + g * lo
                acc_hi = acc_hi + g * hi
            o_ref[rows, :half] = x_ref[rows, :half] + gate_f[:, :half] * acc_lo
            o_ref[rows, half:] = x_ref[rows, half:] + gate_f[:, half:] * acc_hi
            return carry
        lax.fori_loop(0, tc // SUBLANES, body, 0)

        @pl.when(i == n - 1)
        def _():
            wait_step(1 - slot)

    for slot in range(2):
        @pl.when(i % 2 == slot)
        def _(slot=slot):
            step(slot)


def _combine(dest_flat, y, x2, gates, mod_base, mod_l, S, tc=256):
    N, D = x2.shape
    per_b = S // tc
    grid_spec = pltpu.PrefetchScalarGridSpec(
        num_scalar_prefetch=1,
        grid=(N // tc,),
        in_specs=[pl.BlockSpec(memory_space=pl.ANY),
                  pl.BlockSpec((tc, D), lambda i, d: (i, 0)),
                  pl.BlockSpec((tc, LANES), lambda i, d: (i, 0)),
                  pl.BlockSpec((1, N_MOD, D), lambda i, d: (i // per_b, 0, 0)),
                  pl.BlockSpec((N_MOD, D), lambda i, d: (0, 0))],
        out_specs=pl.BlockSpec((tc, D), lambda i, d: (i, 0)),
        scratch_shapes=[pltpu.VMEM((TOP_K, tc, D // 2), jnp.uint32),
                        pltpu.VMEM((TOP_K, tc, D // 2), jnp.uint32), pltpu.SemaphoreType.DMA((2,))],
    )
    return pl.pallas_call(
        functools.partial(_combine_kernel, tc=tc),
        grid_spec=grid_spec,
        out_shape=jax.ShapeDtypeStruct((N, D), F32),
        compiler_params=_params(("arbitrary",)),
        name="moe_combine",
    )(dest_flat, y, x2, gates, mod_base, mod_l)


EXPERT_ROWS = 256


def _rope_tables(positions):
    half = ROT_DIM // 2
    inv_freq = ROPE_THETA ** (-jnp.arange(half, dtype=F32) * 2.0 / ROT_DIM)
    ang = positions.astype(F32).reshape(-1, 1) * inv_freq
    cos, sin = jnp.cos(ang), jnp.sin(ang)
    n = ang.shape[0]
    ones = jnp.ones((n, HEAD_DIM - ROT_DIM), F32)
    zeros = jnp.zeros((n, HEAD_DIM - half), F32)
    cos_t = jnp.concatenate([cos, cos, ones], axis=1)
    s1_t = jnp.concatenate([-sin, zeros], axis=1)
    s2_t = jnp.concatenate([jnp.zeros((n, half), F32), sin,
                            jnp.zeros((n, HEAD_DIM - ROT_DIM), F32)], axis=1)
    return cos_t, s1_t, s2_t


def _routing_tables(top_idx, rank, counts, tm, n_blocks):
    N = top_idx.shape[0]
    nblk = (counts + tm - 1) // tm
    blk_end = jnp.cumsum(nblk)
    blk_start = blk_end - nblk
    dest = blk_start[top_idx] * tm + rank
    owner = jnp.sum((blk_end[None, :] <= jnp.arange(n_blocks, dtype=I32)[:, None]).astype(I32), axis=1)
    block_e = jnp.minimum(owner, N_EXPERTS - 1).astype(I32)
    nused = blk_end[-1:].astype(I32)
    pad_start = (blk_start * tm + counts).astype(I32)
    pad_len = (nblk * tm - counts).astype(I32)
    return dest.reshape(-1).astype(I32), block_e, nused, pad_start, pad_len


def kernel(x, c, positions, w_mod, b_mod, mod_layer, w_in, q_norm, k_norm, w_s, b_s, v_ln_g, v_ln_b,
           out_norm, w_out, w_router, b_router, w_gate_up, b_gate_up, w_down, b_down):
    B, S, D = x.shape
    N = B * S
    depth = w_in.shape[0]
    d_ff = w_down.shape[2]
    tm_e = EXPERT_ROWS
    n_blocks = N * TOP_K // tm_e + N_EXPERTS

    mod_base = _mod_base(c, w_mod, b_mod).reshape(B, N_MOD, D)
    cos_t, s1_t, s2_t = _rope_tables(positions)
    x2 = x.reshape(N, D)

    for l in range(depth):
        mod_l = mod_layer[l]
        h = _mixer_norm(x2, mod_base, mod_l, S)
        qk_gain = jnp.stack([q_norm[l] * HEAD_DIM ** -0.5, k_norm[l]]).reshape(2, 1, HEAD_DIM)
        qk, v, u, gv = _in_proj(h, _to_bf16(w_in, l, 256), qk_gain, cos_t, s1_t, s2_t,
                                v_ln_g[l].reshape(1, D_GMLP), v_ln_b[l].reshape(1, D_GMLP))
        attn = _attention(qk.reshape(B, S, -1), v.reshape(B, S, -1))
        mix = _gmlp_and_norms(attn.reshape(N, D_ATTN), u, gv, w_s[l], jnp.transpose(b_s[l]),
                              out_norm[l].reshape(1, -1))
        x2 = _out_proj(mix, _to_bf16(w_out, l, 512), x2, mod_base, mod_l, S)

        w_router_p = jnp.pad(w_router[l], ((0, 0), (0, LANES - N_EXPERTS)))
        b_router_p = jnp.pad(b_router[l], (0, LANES - N_EXPERTS)).reshape(1, LANES)
        h2, top_idx, gates, rank, counts = _ffn_norm_router(x2, mod_base, mod_l, w_router_p,
                                                            b_router_p, S)
        dest, block_e, nused, pad_start, pad_len = _routing_tables(
            top_idx[:, :TOP_K], rank[:, :TOP_K], counts[0, :N_EXPERTS], tm_e, n_blocks)
        xs = _dispatch(dest, pad_start, pad_len, nused, h2, n_blocks * tm_e, tm_e)
        wg, wu = _split_gate_up(w_gate_up, l)
        bg = b_gate_up[l][:, 0::2].reshape(N_EXPERTS, 1, d_ff)
        bu = b_gate_up[l][:, 1::2].reshape(N_EXPERTS, 1, d_ff)
        wd = _to_bf16(w_down.reshape(depth, N_EXPERTS * d_ff, D), l, 512).reshape(N_EXPERTS, d_ff, D)
        y = _experts(xs, block_e, nused, wg, wu, wd, bg, bu,
                     b_down[l].reshape(N_EXPERTS, 1, D), tm_e)
        x2 = _combine(dest, y, x2, gates, mod_base, mod_l, S)
    return x2.reshape(B, S, D)
```

```python
import functools
import math

import jax
import jax.numpy as jnp
from jax import lax
from jax.experimental import pallas as pl
from jax.experimental.pallas import tpu as pltpu

F32 = jnp.float32
BF16 = jnp.bfloat16
I32 = jnp.int32

HEAD_DIM = 128
N_HEADS_ATTN = 16
N_GROUPS_GMLP = 16
D_ATTN = N_HEADS_ATTN * HEAD_DIM
D_GMLP = N_GROUPS_GMLP * HEAD_DIM
ROT_DIM = HEAD_DIM // 4
ROPE_THETA = 500000.0
DILATED_PATTERNS = ((128, 1), (512, 4), (2048, 16))
GMLP_CHUNK = 128
N_EXPERTS = 32
TOP_K = 4
SWIGLU_LIMIT = 7.0
SWIGLU_ALPHA = 1.702
N_MOD = 6
EPS = 1e-6
LANES = 128
MASKED = -1e30

VMEM_LIMIT = 56 * 1024 * 1024


def _params(sem, vmem=VMEM_LIMIT):
    return pltpu.CompilerParams(dimension_semantics=sem, vmem_limit_bytes=vmem)


def _mod_kernel(c_ref, w_ref, b_ref, o_ref):
    c = c_ref[...]
    s = c * jax.nn.sigmoid(c)
    o_ref[...] = jnp.dot(s.astype(BF16), w_ref[...].astype(BF16),
                         preferred_element_type=F32) + b_ref[...]


def _mod_base(c, w_mod, b_mod, tn=512):
    B, D = c.shape
    n_out = w_mod.shape[1]
    return pl.pallas_call(
        _mod_kernel,
        grid=(n_out // tn,),
        in_specs=[pl.BlockSpec((B, D), lambda j: (0, 0)),
                  pl.BlockSpec((D, tn), lambda j: (0, j)),
                  pl.BlockSpec((1, tn), lambda j: (0, j))],
        out_specs=pl.BlockSpec((B, tn), lambda j: (0, j)),
        out_shape=jax.ShapeDtypeStruct((B, n_out), F32),
        compiler_params=_params(("parallel",)),
        name="mod_base",
    )(c, w_mod, b_mod.reshape(1, n_out))


def _pack_halves(x):
    c = x.shape[1] // 2
    bits = pltpu.bitcast(x.astype(BF16).astype(F32), jnp.uint32)
    return (bits[:, c:] & jnp.uint32(0xFFFF0000)) | (bits[:, :c] >> 16)


def _unpack_halves(p):
    lo = pltpu.bitcast(p << 16, F32)
    hi = pltpu.bitcast(p & jnp.uint32(0xFFFF0000), F32)
    return lo, hi


def _mod_row(mb_ref, ml_ref, idx):
    return mb_ref[0, idx:idx + 1, :] + ml_ref[idx:idx + 1, :]


def _norm_mod(x, mb_ref, ml_ref, shift_idx, scale_idx):
    y = x * lax.rsqrt(jnp.mean(x * x, axis=-1, keepdims=True) + EPS)
    return y * (1.0 + _mod_row(mb_ref, ml_ref, scale_idx)) + _mod_row(mb_ref, ml_ref, shift_idx)


def _norm_kernel(x_ref, mb_ref, ml_ref, h_ref):
    h_ref[...] = _norm_mod(x_ref[...], mb_ref, ml_ref, 0, 1).astype(h_ref.dtype)


def _mixer_norm(x2, mod_base, mod_l, S, tm=256):
    N, D = x2.shape
    per_b = S // tm
    return pl.pallas_call(
        _norm_kernel,
        grid=(N // tm,),
        in_specs=[pl.BlockSpec((tm, D), lambda i: (i, 0)),
                  pl.BlockSpec((1, N_MOD, D), lambda i: (i // per_b, 0, 0)),
                  pl.BlockSpec((N_MOD, D), lambda i: (0, 0))],
        out_specs=pl.BlockSpec((tm, D), lambda i: (i, 0)),
        out_shape=jax.ShapeDtypeStruct((N, D), BF16),
        compiler_params=_params(("parallel",)),
        name="mixer_norm",
    )(x2, mod_base, mod_l)


def _gelu(x):
    return 0.5 * x * (1.0 + lax.erf(x * (1.0 / math.sqrt(2.0))))


def _proj_kernel(*refs, mode, sub):
    if mode == "gelu_ln":
        h_ref, w_ref, lg_ref, lb_ref, o_ref = refs
    else:
        h_ref, w_ref, o_ref = refs
    bn = o_ref.shape[1]
    for s in range(bn // sub):
        acc = jnp.dot(h_ref[...], w_ref[:, s * sub:(s + 1) * sub], preferred_element_type=F32)
        for hh in range(sub // HEAD_DIM):
            t = acc[:, hh * HEAD_DIM:(hh + 1) * HEAD_DIM]
            sl = slice(s * sub + hh * HEAD_DIM, s * sub + (hh + 1) * HEAD_DIM)
            if mode == "gelu":
                r = _gelu(t)
            elif mode == "gelu_ln":
                g = _gelu(t)
                gc = g - jnp.mean(g, axis=-1, keepdims=True)
                r = gc * lax.rsqrt(jnp.mean(gc * gc, axis=-1, keepdims=True) + EPS)
                r = r * lg_ref[:, sl] + lb_ref[:, sl]
            else:
                r = t
            o_ref[:, sl] = r.astype(o_ref.dtype)


def _proj(h, w, col0, width, mode, extra=(), extra_specs=(), bm=1024, bn=1024, sub=256):
    N, D = h.shape
    j0 = col0 // bn
    return pl.pallas_call(
        functools.partial(_proj_kernel, mode=mode, sub=sub),
        grid=(N // bm, width // bn),
        in_specs=[pl.BlockSpec((bm, D), lambda i, j: (i, 0)),
                  pl.BlockSpec((D, bn), lambda i, j: (0, j0 + j))] + list(extra_specs),
        out_specs=pl.BlockSpec((bm, bn), lambda i, j: (i, j)),
        out_shape=jax.ShapeDtypeStruct((N, width), BF16),
        compiler_params=_params(("parallel", "arbitrary")),
        name="in_proj_" + mode,
    )(h, w, *extra)


def _qk_proj_kernel(h_ref, w_ref, gain_ref, cos_ref, s1_ref, s2_ref, o_ref, acc0, acc1):
    t = pl.program_id(0)
    bn = o_ref.shape[1]

    @pl.when(t == 0)
    def _():
        acc1[...] = jnp.zeros_like(acc1)

    def step(acc_mm, acc_ep):
        acc_mm[...] = jnp.dot(h_ref[...], w_ref[...], preferred_element_type=F32)
        for hh in range(bn // HEAD_DIM):
            sl = slice(hh * HEAD_DIM, (hh + 1) * HEAD_DIM)
            a = acc_ep[:, sl]
            y = a * lax.rsqrt(jnp.mean(a * a, axis=-1, keepdims=True) + EPS) * gain_ref[0]
            r = (y * cos_ref[...] + pltpu.roll(y, HEAD_DIM - ROT_DIM // 2, 1) * s1_ref[...]
                 + pltpu.roll(y, ROT_DIM // 2, 1) * s2_ref[...])
            o_ref[:, sl] = r.astype(o_ref.dtype)

    @pl.when(t % 2 == 0)
    def _():
        step(acc0, acc1)

    @pl.when(t % 2 == 1)
    def _():
        step(acc1, acc0)


def _qk_proj(h, w, qk_gain, cos_t, s1_t, s2_t, bm=1024, bn=512):
    N, D = h.shape
    width = 2 * D_ATTN
    nj = width // bn
    n_tiles = (N // bm) * nj
    tiles_per_part = D_ATTN // bn
    mm = lambda t: jnp.minimum(t, n_tiles - 1)
    ep = lambda t: jnp.maximum(t - 1, 0)
    tok_tab = pl.BlockSpec((bm, HEAD_DIM), lambda t: (ep(t) // nj, 0))
    return pl.pallas_call(
        _qk_proj_kernel,
        grid=(n_tiles + 1,),
        in_specs=[pl.BlockSpec((bm, D), lambda t: (mm(t) // nj, 0)),
                  pl.BlockSpec((D, bn), lambda t: (0, mm(t) % nj)),
                  pl.BlockSpec((1, 1, HEAD_DIM), lambda t: ((ep(t) % nj) // tiles_per_part, 0, 0)),
                  tok_tab, tok_tab, tok_tab],
        out_specs=pl.BlockSpec((bm, bn), lambda t: (ep(t) // nj, ep(t) % nj)),
        out_shape=jax.ShapeDtypeStruct((N, width), BF16),
        scratch_shapes=[pltpu.VMEM((bm, bn), F32), pltpu.VMEM((bm, bn), F32)],
        compiler_params=_params(("arbitrary",)),
        name="in_proj_qk",
    )(h, w, qk_gain, cos_t, s1_t, s2_t)


def _cast_kernel(x_ref, o_ref):
    o_ref[...] = x_ref[0].astype(o_ref.dtype)


def _to_bf16(w3, layer, tr):
    _, R, C = w3.shape
    return pl.pallas_call(
        _cast_kernel,
        grid=(R // tr,),
        in_specs=[pl.BlockSpec((1, tr, C), lambda i: (layer, i, 0))],
        out_specs=pl.BlockSpec((tr, C), lambda i: (i, 0)),
        out_shape=jax.ShapeDtypeStruct((R, C), BF16),
        compiler_params=_params(("parallel",)),
        name="to_bf16",
    )(w3)


def _in_proj(h, w_in, qk_gain, cos_t, s1_t, s2_t, ln_g, ln_b, bm=1024, bn=1024):
    qk = _qk_proj(h, w_in, qk_gain, cos_t, s1_t, s2_t)
    v = _proj(h, w_in, 2 * D_ATTN, D_ATTN, "none", bm=bm, bn=bn)
    u = _proj(h, w_in, 3 * D_ATTN, D_GMLP, "gelu", bm=bm, bn=bn)
    row = pl.BlockSpec((1, bn), lambda i, j: (0, j))
    gv = _proj(h, w_in, 3 * D_ATTN + D_GMLP, D_GMLP, "gelu_ln", (ln_g, ln_b), (row, row), bm, bn)
    return qk, v, u, gv


def _attn_kernel(q_ref, k_ref, v_ref, bias_ref, o_ref, *, tq):
    S = q_ref.shape[1]
    n_q = S // tq
    for i in range(n_q):
        kend = (i + 1) * tq
        q = q_ref[0, i * tq:(i + 1) * tq, :]
        s = lax.dot_general(q, k_ref[0, :kend, :], (((1,), (1,)), ((), ())),
                            preferred_element_type=F32)
        off = (n_q - 1 - i) * tq
        s = s + bias_ref[:, off:off + kend]
        m = jnp.max(s, axis=-1, keepdims=True)
        p = jnp.exp(s - m)
        l = jnp.sum(p, axis=-1, keepdims=True)
        o = jnp.dot(p.astype(BF16), v_ref[0, :kend, :], preferred_element_type=F32)
        o_ref[0, i * tq:(i + 1) * tq, :] = (o / l).astype(o_ref.dtype)


def _attn_bias_table(S, tq):
    r = jnp.arange(tq, dtype=I32)[:, None]
    u = jnp.arange(S, dtype=I32)[None, :]
    d = r + (S - tq) - u
    mult = jnp.zeros((tq, S), F32)
    for window, dil in DILATED_PATTERNS:
        mult = mult + ((d >= 0) & (d % dil == 0) & (d <= window)).astype(F32)
    return jnp.where(mult > 0, jnp.log(jnp.maximum(mult, 1.0)), MASKED)


def _attention(qk3, v3, tq=256):
    B, S, _ = v3.shape
    bias = _attn_bias_table(S, tq)
    H = N_HEADS_ATTN
    return pl.pallas_call(
        functools.partial(_attn_kernel, tq=tq),
        grid=(B, H),
        in_specs=[pl.BlockSpec((1, S, HEAD_DIM), lambda b, h: (b, 0, h)),
                  pl.BlockSpec((1, S, HEAD_DIM), lambda b, h: (b, 0, H + h)),
                  pl.BlockSpec((1, S, HEAD_DIM), lambda b, h: (b, 0, h)),
                  pl.BlockSpec((tq, S), lambda b, h: (0, 0))],
        out_specs=pl.BlockSpec((1, S, HEAD_DIM), lambda b, h: (b, 0, h)),
        out_shape=jax.ShapeDtypeStruct((B, S, D_ATTN), BF16),
        compiler_params=_params(("parallel", "parallel")),
        name="dilated_attn",
    )(qk3, qk3, v3, bias)


def _gmlp_kernel(a_ref, u_ref, gv_ref, ws_ref, bst_ref, on_ref, o_ref, g_scr):
    tg = a_ref.shape[0]
    row = lax.broadcasted_iota(I32, (GMLP_CHUNK, GMLP_CHUNK), 0)
    col = lax.broadcasted_iota(I32, (GMLP_CHUNK, GMLP_CHUNK), 1)
    causal = row >= col
    for g in range(N_GROUPS_GMLP):
        w = jnp.where(causal, ws_ref[g], 0.0).astype(BF16)
        bias = bst_ref[:, g:g + 1]
        sl = slice(g * HEAD_DIM, (g + 1) * HEAD_DIM)
        for c in range(tg // GMLP_CHUNK):
            rows = slice(c * GMLP_CHUNK, (c + 1) * GMLP_CHUNK)
            mixed = jnp.dot(w, gv_ref[rows, sl], preferred_element_type=F32) + bias
            g_scr[rows, sl] = u_ref[rows, sl].astype(F32) * mixed
    gm = g_scr[...]
    gm = gm * lax.rsqrt(jnp.mean(gm * gm, axis=-1, keepdims=True) + EPS)
    o_ref[:, D_ATTN:] = (gm * on_ref[:, D_ATTN:]).astype(o_ref.dtype)
    a = a_ref[...].astype(F32)
    a = a * lax.rsqrt(jnp.mean(a * a, axis=-1, keepdims=True) + EPS)
    o_ref[:, :D_ATTN] = (a * on_ref[:, :D_ATTN]).astype(o_ref.dtype)


def _gmlp_and_norms(attn2, u, gv, w_s, b_s_t, out_norm, tg=512):
    N = attn2.shape[0]
    D_mix = D_ATTN + D_GMLP
    return pl.pallas_call(
        _gmlp_kernel,
        grid=(N // tg,),
        in_specs=[pl.BlockSpec((tg, D_ATTN), lambda i: (i, 0)),
                  pl.BlockSpec((tg, D_GMLP), lambda i: (i, 0)),
                  pl.BlockSpec((tg, D_GMLP), lambda i: (i, 0)),
                  pl.BlockSpec((N_GROUPS_GMLP, GMLP_CHUNK, GMLP_CHUNK), lambda i: (0, 0, 0)),
                  pl.BlockSpec((GMLP_CHUNK, N_GROUPS_GMLP), lambda i: (0, 0)),
                  pl.BlockSpec((1, D_mix), lambda i: (0, 0))],
        out_specs=pl.BlockSpec((tg, D_mix), lambda i: (i, 0)),
        out_shape=jax.ShapeDtypeStruct((N, D_mix), BF16),
        scratch_shapes=[pltpu.VMEM((tg, D_GMLP), F32)],
        compiler_params=_params(("parallel",)),
        name="gmlp_norms",
    )(attn2, u, gv, w_s, b_s_t, out_norm)


def _outproj_kernel(m_ref, w_ref, x_ref, mb_ref, ml_ref, o_ref):
    acc = jnp.dot(m_ref[...], w_ref[...], preferred_element_type=F32)
    o_ref[...] = x_ref[...] + _mod_row(mb_ref, ml_ref, 2) * acc


def _out_proj(mix, w_out, x2, mod_base, mod_l, S, bm=1024, bn=512):
    N, D_mix = mix.shape
    D = w_out.shape[1]
    per_b = S // bm
    return pl.pallas_call(
        _outproj_kernel,
        grid=(N // bm, D // bn),
        in_specs=[pl.BlockSpec((bm, D_mix), lambda i, j: (i, 0)),
                  pl.BlockSpec((D_mix, bn), lambda i, j: (0, j)),
                  pl.BlockSpec((bm, bn), lambda i, j: (i, j)),
                  pl.BlockSpec((1, N_MOD, bn), lambda i, j: (i // per_b, 0, j)),
                  pl.BlockSpec((N_MOD, bn), lambda i, j: (0, j))],
        out_specs=pl.BlockSpec((bm, bn), lambda i, j: (i, j)),
        out_shape=jax.ShapeDtypeStruct((N, D), F32),
        compiler_params=_params(("parallel", "arbitrary")),
        name="out_proj",
    )(mix, w_out, x2, mod_base, mod_l)


def _router_kernel(x_ref, mb_ref, ml_ref, wr_ref, br_ref,
                   h_ref, idx_ref, gate_ref, rank_ref, cnt_ref, cnt_scr):
    i = pl.program_id(0)
    tm = x_ref.shape[0]

    @pl.when(i == 0)
    def _():
        cnt_scr[...] = jnp.zeros_like(cnt_scr)

    h = _norm_mod(x_ref[...], mb_ref, ml_ref, 3, 4)
    h_ref[...] = _pack_halves(h)
    logits = jnp.dot(h.astype(BF16), wr_ref[...].astype(BF16),
                     preferred_element_type=F32) + br_ref[...]
    lane = lax.broadcasted_iota(I32, (tm, LANES), 1)
    lane_f = lane.astype(F32)
    vals = jnp.where(lane < N_EXPERTS, logits, -jnp.inf)
    tops, sels = [], []
    idx_out = jnp.zeros((tm, LANES), F32)
    for k in range(TOP_K):
        m = jnp.max(vals, axis=-1, keepdims=True)
        first = jnp.min(jnp.where(vals == m, lane_f, float(LANES)), axis=-1, keepdims=True)
        sel = lane_f == first
        tops.append(m)
        sels.append(sel)
        idx_out = jnp.where(lane == k, first, idx_out)
        vals = jnp.where(sel, -jnp.inf, vals)
    exps = [jnp.exp(t - tops[0]) for t in tops]
    denom = exps[0] + exps[1] + exps[2] + exps[3]
    gate_out = jnp.zeros((tm, LANES), F32)
    for k in range(TOP_K):
        gate_out = jnp.where(lane == k, exps[k] / denom, gate_out)

    chosen = jnp.zeros((tm, LANES), F32)
    for sel in sels:
        chosen = chosen + sel.astype(F32)
    r = lax.broadcasted_iota(I32, (tm, tm), 0)
    c = lax.broadcasted_iota(I32, (tm, tm), 1)
    earlier = jnp.where(r > c, 1.0, 0.0).astype(BF16)
    before = jnp.dot(earlier, chosen.astype(BF16), preferred_element_type=F32) + cnt_scr[...]
    rank_out = jnp.zeros((tm, LANES), F32)
    for k in range(TOP_K):
        rk = jnp.sum(jnp.where(sels[k], before, 0.0), axis=-1, keepdims=True)
        rank_out = jnp.where(lane == k, rk, rank_out)
    cnt_scr[...] = cnt_scr[...] + jnp.sum(chosen, axis=0, keepdims=True)

    idx_ref[...] = idx_out.astype(I32)
    gate_ref[...] = gate_out
    rank_ref[...] = rank_out.astype(I32)
    cnt_ref[...] = cnt_scr[...].astype(I32)


def _ffn_norm_router(x2, mod_base, mod_l, w_router_p, b_router_p, S, tm=256):
    N, D = x2.shape
    per_b = S // tm
    tok_lane = pl.BlockSpec((tm, LANES), lambda i: (i, 0))
    return pl.pallas_call(
        _router_kernel,
        grid=(N // tm,),
        in_specs=[pl.BlockSpec((tm, D), lambda i: (i, 0)),
                  pl.BlockSpec((1, N_MOD, D), lambda i: (i // per_b, 0, 0)),
                  pl.BlockSpec((N_MOD, D), lambda i: (0, 0)),
                  pl.BlockSpec((D, LANES), lambda i: (0, 0)),
                  pl.BlockSpec((1, LANES), lambda i: (0, 0))],
        out_specs=[pl.BlockSpec((tm, D // 2), lambda i: (i, 0)), tok_lane, tok_lane, tok_lane,
                   pl.BlockSpec((1, LANES), lambda i: (0, 0))],
        out_shape=[jax.ShapeDtypeStruct((N, D // 2), jnp.uint32),
                   jax.ShapeDtypeStruct((N, LANES), I32),
                   jax.ShapeDtypeStruct((N, LANES), F32),
                   jax.ShapeDtypeStruct((N, LANES), I32),
                   jax.ShapeDtypeStruct((1, LANES), I32)],
        scratch_shapes=[pltpu.VMEM((1, LANES), F32)],
        compiler_params=_params(("arbitrary",)),
        name="ffn_norm_router",
    )(x2, mod_base, mod_l, w_router_p, b_router_p)


MXU_DIM = 256


def _split_gate_up_kernel(w_ref, g_ref, u_ref):
    r = lax.broadcasted_iota(I32, (MXU_DIM, MXU_DIM), 0)
    c = lax.broadcasted_iota(I32, (MXU_DIM, MXU_DIM), 1)
    src = jnp.where(c < LANES, 2 * c, 2 * (c - LANES) + 1)
    perm = jnp.where(r == src, 1.0, 0.0).astype(BF16)
    for ch in range(w_ref.shape[3] // MXU_DIM):
        w = w_ref[0, 0, :, ch * MXU_DIM:(ch + 1) * MXU_DIM].astype(BF16)
        res = jnp.dot(w, perm, preferred_element_type=F32)
        g_ref[0, :, ch * LANES:(ch + 1) * LANES] = res[:, :LANES].astype(BF16)
        u_ref[0, :, ch * LANES:(ch + 1) * LANES] = res[:, LANES:].astype(BF16)


def _split_gate_up(w_gate_up, layer, tr=1024):
    _, E, D, two_ff = w_gate_up.shape
    d_ff = two_ff // 2
    out = jax.ShapeDtypeStruct((E, D, d_ff), BF16)
    return pl.pallas_call(
        _split_gate_up_kernel,
        grid=(E, D // tr),
        in_specs=[pl.BlockSpec((1, 1, tr, two_ff), lambda e, i: (layer, e, i, 0))],
        out_specs=[pl.BlockSpec((1, tr, d_ff), lambda e, i: (e, i, 0)),
                   pl.BlockSpec((1, tr, d_ff), lambda e, i: (e, i, 0))],
        out_shape=[out, out],
        compiler_params=_params(("parallel", "parallel")),
        name="split_gate_up",
    )(w_gate_up)


def _dispatch_kernel(dest_ref, pad_start_ref, pad_len_ref, nused_ref, h_ref, xs_hbm,
                     zero_buf, sem, pad_sem, blk_sem, *, td, tm):
    i = pl.program_id(0)
    n = pl.num_programs(0)
    n_blocks = xs_hbm.shape[0] // tm

    def body(r, carry):
        base = (i * td + r) * TOP_K
        for k in range(TOP_K):
            pltpu.make_async_copy(h_ref.at[pl.ds(r, 1), :], xs_hbm.at[pl.ds(dest_ref[base + k], 1), :],
                                  sem).start()
        return carry
    lax.fori_loop(0, td, body, 0, unroll=2)

    for k in range(TOP_K):
        pltpu.make_async_copy(h_ref, xs_hbm.at[pl.ds(0, td), :], sem).wait()

    @pl.when(i == n - 1)
    def _():
        zero_buf[...] = jnp.zeros_like(zero_buf)

        def per_expert(e, total):
            start = pad_start_ref[e]
            count = pad_len_ref[e]

            def fill(r, carry):
                pltpu.make_async_copy(zero_buf.at[pl.ds(0, 1), :], xs_hbm.at[pl.ds(start + r, 1), :],
                                      pad_sem).start()
                return carry
            lax.fori_loop(0, count, fill, 0)
            return total + count
        total = lax.fori_loop(0, N_EXPERTS, per_expert, 0)

        def block_copy(blk):
            return pltpu.make_async_copy(zero_buf, xs_hbm.at[pl.ds(pl.multiple_of(blk * tm, tm), tm), :],
                                         blk_sem)

        def fill_block(blk, carry):
            block_copy(blk).start()
            return carry
        lax.fori_loop(nused_ref[0], n_blocks, fill_block, 0)

        def retire(r, carry):
            pltpu.make_async_copy(zero_buf.at[pl.ds(0, 1), :], xs_hbm.at[pl.ds(0, 1), :], pad_sem).wait()
            return carry
        lax.fori_loop(0, total, retire, 0)

        def retire_block(blk, carry):
            block_copy(blk).wait()
            return carry
        lax.fori_loop(nused_ref[0], n_blocks, retire_block, 0)


def _dispatch(dest_flat, pad_start, pad_len, nused, h2, n_slots, tm, td=1024):
    N, half = h2.shape
    grid_spec = pltpu.PrefetchScalarGridSpec(
        num_scalar_prefetch=4,
        grid=(N // td,),
        in_specs=[pl.BlockSpec((td, half), lambda i, d, ps, pn, nu: (i, 0))],
        out_specs=pl.BlockSpec(memory_space=pl.ANY),
        scratch_shapes=[pltpu.VMEM((tm, half), jnp.uint32), pltpu.SemaphoreType.DMA(()),
                        pltpu.SemaphoreType.DMA(()), pltpu.SemaphoreType.DMA(())],
    )
    return pl.pallas_call(
        functools.partial(_dispatch_kernel, td=td, tm=tm),
        grid_spec=grid_spec,
        out_shape=jax.ShapeDtypeStruct((n_slots, half), jnp.uint32),
        compiler_params=_params(("arbitrary",)),
        name="moe_dispatch",
    )(dest_flat, pad_start, pad_len, nused, h2)


def _expert_kernel(be_ref, nused_ref, x_ref, wg_ref, wu_ref, wd_ref, bg_ref, bu_ref, bd_ref, y_ref):
    b = pl.program_id(0)

    @pl.when(b < nused_ref[0])
    def _():
        lo, hi = _unpack_halves(x_ref[...])
        x = jnp.concatenate([lo.astype(BF16), hi.astype(BF16)], axis=1)
        gate = jnp.dot(x, wg_ref[0], preferred_element_type=F32) + bg_ref[0]
        up = jnp.dot(x, wu_ref[0], preferred_element_type=F32) + bu_ref[0]
        gate = jnp.minimum(gate, SWIGLU_LIMIT)
        up = jnp.clip(up, -SWIGLU_LIMIT, SWIGLU_LIMIT)
        act = (up + 1.0) * gate * jax.nn.sigmoid(SWIGLU_ALPHA * gate)
        y = jnp.dot(act.astype(BF16), wd_ref[0], preferred_element_type=F32) + bd_ref[0]
        y_ref[...] = _pack_halves(y)

    @pl.when(b >= nused_ref[0])
    def _():
        y_ref[...] = jnp.zeros_like(y_ref)


def _experts(xs, block_e, nused, wg, wu, wd, bg, bu, bd, tm):
    half = xs.shape[1]
    D = 2 * half
    n_blocks = block_e.shape[0]
    d_ff = wg.shape[2]
    e_map = lambda b, be, nu: (be[b], 0, 0)
    x_map = lambda b, be, nu: (jnp.minimum(b, nu[0] - 1), 0)
    grid_spec = pltpu.PrefetchScalarGridSpec(
        num_scalar_prefetch=2,
        grid=(n_blocks,),
        in_specs=[pl.BlockSpec((tm, half), x_map),
                  pl.BlockSpec((1, D, d_ff), e_map),
                  pl.BlockSpec((1, D, d_ff), e_map),
                  pl.BlockSpec((1, d_ff, D), e_map),
                  pl.BlockSpec((1, 1, d_ff), e_map),
                  pl.BlockSpec((1, 1, d_ff), e_map),
                  pl.BlockSpec((1, 1, D), e_map)],
        out_specs=pl.BlockSpec((tm, half), lambda b, be, nu: (b, 0)),
    )
    return pl.pallas_call(
        _expert_kernel,
        grid_spec=grid_spec,
        out_shape=jax.ShapeDtypeStruct((n_blocks * tm, half), jnp.uint32),
        compiler_params=_params(("arbitrary",)),
        name="experts",
    )(block_e, nused, xs, wg, wu, wd, bg, bu, bd)


SUBLANES = 8


def _combine_kernel(dest_ref, y_hbm, x_ref, gate_ref, mb_ref, ml_ref, o_ref, buf0, buf1, sem, *, tc):
    i = pl.program_id(0)
    n = pl.num_programs(0)
    bufs = (buf0, buf1)
    half = buf0.shape[2]

    def row_copy(d, slot, k, r):
        return pltpu.make_async_copy(y_hbm.at[pl.ds(d, 1), :],
                                     bufs[slot].at[k, pl.ds(r, 1), :], sem.at[slot])

    def issue_rows(step, slot, r0, count):
        for rr in range(count):
            base = (step * tc + r0 + rr) * TOP_K
            for k in range(TOP_K):
                row_copy(dest_ref[base + k], slot, k, r0 + rr).start()

    def wait_step(slot):
        for k in range(TOP_K):
            pltpu.make_async_copy(y_hbm.at[pl.ds(0, tc), :], bufs[slot].at[k], sem.at[slot]).wait()

    @pl.when(i == 0)
    def _():
        def body(c, carry):
            issue_rows(0, 0, c * SUBLANES, SUBLANES)
            return carry
        lax.fori_loop(0, tc // SUBLANES, body, 0)

    nxt = jnp.minimum(i + 1, n - 1)
    gate_f = _mod_row(mb_ref, ml_ref, 5)

    def step(slot):
        wait_step(slot)

        def body(c, carry):
            r0 = pl.multiple_of(c * SUBLANES, SUBLANES)
            issue_rows(nxt, 1 - slot, r0, SUBLANES)
            rows = pl.ds(r0, SUBLANES)
            acc_lo = jnp.zeros((SUBLANES, half), F32)
            acc_hi = jnp.zeros((SUBLANES, half), F32)
            for k in range(TOP_K):
                lo, hi = _unpack_halves(bufs[slot][k, rows, :])
                g = gate_ref[rows, k:k + 1]
                acc_lo = acc_lo + g * lo
                acc_hi = acc_hi + g * hi
            o_ref[rows, :half] = x_ref[rows, :half] + gate_f[:, :half] * acc_lo
            o_ref[rows, half:] = x_ref[rows, half:] + gate_f[:, half:] * acc_hi
            return carry
        lax.fori_loop(0, tc // SUBLANES, body, 0)

        @pl.when(i == n - 1)
        def _():
            wait_step(1 - slot)

    for slot in range(2):
        @pl.when(i % 2 == slot)
        def _(slot=slot):
            step(slot)


def _combine(dest_flat, y, x2, gates, mod_base, mod_l, S, tc=256):
    N, D = x2.shape
    per_b = S // tc
    grid_spec = pltpu.PrefetchScalarGridSpec(
        num_scalar_prefetch=1,
        grid=(N // tc,),
        in_specs=[pl.BlockSpec(memory_space=pl.ANY),
                  pl.BlockSpec((tc, D), lambda i, d: (i, 0)),
                  pl.BlockSpec((tc, LANES), lambda i, d: (i, 0)),
                  pl.BlockSpec((1, N_MOD, D), lambda i, d: (i // per_b, 0, 0)),
                  pl.BlockSpec((N_MOD, D), lambda i, d: (0, 0))],
        out_specs=pl.BlockSpec((tc, D), lambda i, d: (i, 0)),
        scratch_shapes=[pltpu.VMEM((TOP_K, tc, D // 2), jnp.uint32),
                        pltpu.VMEM((TOP_K, tc, D // 2), jnp.uint32), pltpu.SemaphoreType.DMA((2,))],
    )
    return pl.pallas_call(
        functools.partial(_combine_kernel, tc=tc),
        grid_spec=grid_spec,
        out_shape=jax.ShapeDtypeStruct((N, D), F32),
        compiler_params=_params(("arbitrary",)),
        name="moe_combine",
    )(dest_flat, y, x2, gates, mod_base, mod_l)


EXPERT_ROWS = 256


def _rope_tables(positions):
    half = ROT_DIM // 2
    inv_freq = ROPE_THETA ** (-jnp.arange(half, dtype=F32) * 2.0 / ROT_DIM)
    ang = positions.astype(F32).reshape(-1, 1) * inv_freq
    cos, sin = jnp.cos(ang), jnp.sin(ang)
    n = ang.shape[0]
    ones = jnp.ones((n, HEAD_DIM - ROT_DIM), F32)
    zeros = jnp.zeros((n, HEAD_DIM - half), F32)
    cos_t = jnp.concatenate([cos, cos, ones], axis=1)
    s1_t = jnp.concatenate([-sin, zeros], axis=1)
    s2_t = jnp.concatenate([jnp.zeros((n, half), F32), sin,
                            jnp.zeros((n, HEAD_DIM - ROT_DIM), F32)], axis=1)
    return cos_t, s1_t, s2_t


def _routing_tables(top_idx, rank, counts, tm, n_blocks):
    N = top_idx.shape[0]
    nblk = (counts + tm - 1) // tm
    blk_end = jnp.cumsum(nblk)
    blk_start = blk_end - nblk
    dest = blk_start[top_idx] * tm + rank
    owner = jnp.sum((blk_end[None, :] <= jnp.arange(n_blocks, dtype=I32)[:, None]).astype(I32), axis=1)
    block_e = jnp.minimum(owner, N_EXPERTS - 1).astype(I32)
    nused = blk_end[-1:].astype(I32)
    pad_start = (blk_start * tm + counts).astype(I32)
    pad_len = (nblk * tm - counts).astype(I32)
    return dest.reshape(-1).astype(I32), block_e, nused, pad_start, pad_len


def kernel(x, c, positions, w_mod, b_mod, mod_layer, w_in, q_norm, k_norm, w_s, b_s, v_ln_g, v_ln_b,
           out_norm, w_out, w_router, b_router, w_gate_up, b_gate_up, w_down, b_down):
    B, S, D = x.shape
    N = B * S
    depth = w_in.shape[0]
    d_ff = w_down.shape[2]
    tm_e = EXPERT_ROWS
    n_blocks = N * TOP_K // tm_e + N_EXPERTS

    mod_base = _mod_base(c, w_mod, b_mod).reshape(B, N_MOD, D)
    cos_t, s1_t, s2_t = _rope_tables(positions)
    x2 = x.reshape(N, D)

    for l in range(depth):
        mod_l = mod_layer[l]
        h = _mixer_norm(x2, mod_base, mod_l, S)
        qk_gain = jnp.stack([q_norm[l] * HEAD_DIM ** -0.5, k_norm[l]]).reshape(2, 1, HEAD_DIM)
        qk, v, u, gv = _in_proj(h, _to_bf16(w_in, l, 256), qk_gain, cos_t, s1_t, s2_t,
                                v_ln_g[l].reshape(1, D_GMLP), v_ln_b[l].reshape(1, D_GMLP))
        attn = _attention(qk.reshape(B, S, -1), v.reshape(B, S, -1))
        mix = _gmlp_and_norms(attn.reshape(N, D_ATTN), u, gv, w_s[l], jnp.transpose(b_s[l]),
                              out_norm[l].reshape(1, -1))
        x2 = _out_proj(mix, _to_bf16(w_out, l, 512), x2, mod_base, mod_l, S)

        w_router_p = jnp.pad(w_router[l], ((0, 0), (0, LANES - N_EXPERTS)))
        b_router_p = jnp.pad(b_router[l], (0, LANES - N_EXPERTS)).reshape(1, LANES)
        h2, top_idx, gates, rank, counts = _ffn_norm_router(x2, mod_base, mod_l, w_router_p,
                                                            b_router_p, S)
        dest, block_e, nused, pad_start, pad_len = _routing_tables(
            top_idx[:, :TOP_K], rank[:, :TOP_K], counts[0, :N_EXPERTS], tm_e, n_blocks)
        xs = _dispatch(dest, pad_start, pad_len, nused, h2, n_blocks * tm_e, tm_e)
        wg, wu = _split_gate_up(w_gate_up, l)
        bg = b_gate_up[l][:, 0::2].reshape(N_EXPERTS, 1, d_ff)
        bu = b_gate_up[l][:, 1::2].reshape(N_EXPERTS, 1, d_ff)
        wd = _to_bf16(w_down.reshape(depth, N_EXPERTS * d_ff, D), l, 512).reshape(N_EXPERTS, d_ff, D)
        y = _experts(xs, block_e, nused, wg, wu, wd, bg, bu,
                     b_down[l].reshape(N_EXPERTS, 1, D), tm_e)
        x2 = _combine(dest, y, x2, gates, mod_base, mod_l, S)
    return x2.reshape(B, S, D)
```

```python
import functools
import math

import jax
import jax.numpy as jnp
from jax import lax
from jax.experimental import pallas as pl
from jax.experimental.pallas import tpu as pltpu

F32 = jnp.float32
BF16 = jnp.bfloat16
I32 = jnp.int32

HEAD_DIM = 128
N_HEADS_ATTN = 16
N_GROUPS_GMLP = 16
D_ATTN = N_HEADS_ATTN * HEAD_DIM
D_GMLP = N_GROUPS_GMLP * HEAD_DIM
ROT_DIM = HEAD_DIM // 4
ROPE_THETA = 500000.0
DILATED_PATTERNS = ((128, 1), (512, 4), (2048, 16))
GMLP_CHUNK = 128
N_EXPERTS = 32
TOP_K = 4
SWIGLU_LIMIT = 7.0
SWIGLU_ALPHA = 1.702
N_MOD = 6
EPS = 1e-6
LANES = 128
MASKED = -1e30

VMEM_LIMIT = 56 * 1024 * 1024


def _params(sem, vmem=VMEM_LIMIT):
    return pltpu.CompilerParams(dimension_semantics=sem, vmem_limit_bytes=vmem)


def _mod_kernel(c_ref, w_ref, b_ref, o_ref):
    c = c_ref[...]
    s = c * jax.nn.sigmoid(c)
    o_ref[...] = jnp.dot(s.astype(BF16), w_ref[...].astype(BF16),
                         preferred_element_type=F32) + b_ref[...]


def _mod_base(c, w_mod, b_mod, tn=512):
    B, D = c.shape
    n_out = w_mod.shape[1]
    return pl.pallas_call(
        _mod_kernel,
        grid=(n_out // tn,),
        in_specs=[pl.BlockSpec((B, D), lambda j: (0, 0)),
                  pl.BlockSpec((D, tn), lambda j: (0, j)),
                  pl.BlockSpec((1, tn), lambda j: (0, j))],
        out_specs=pl.BlockSpec((B, tn), lambda j: (0, j)),
        out_shape=jax.ShapeDtypeStruct((B, n_out), F32),
        compiler_params=_params(("parallel",)),
        name="mod_base",
    )(c, w_mod, b_mod.reshape(1, n_out))


def _pack_halves(x):
    c = x.shape[1] // 2
    bits = pltpu.bitcast(x.astype(BF16).astype(F32), jnp.uint32)
    return (bits[:, c:] & jnp.uint32(0xFFFF0000)) | (bits[:, :c] >> 16)


def _unpack_halves(p):
    lo = pltpu.bitcast(p << 16, F32)
    hi = pltpu.bitcast(p & jnp.uint32(0xFFFF0000), F32)
    return lo, hi


def _mod_row(mb_ref, ml_ref, idx):
    return mb_ref[0, idx:idx + 1, :] + ml_ref[idx:idx + 1, :]


def _norm_mod(x, mb_ref, ml_ref, shift_idx, scale_idx):
    y = x * lax.rsqrt(jnp.mean(x * x, axis=-1, keepdims=True) + EPS)
    return y * (1.0 + _mod_row(mb_ref, ml_ref, scale_idx)) + _mod_row(mb_ref, ml_ref, shift_idx)


def _norm_kernel(x_ref, mb_ref, ml_ref, h_ref):
    h_ref[...] = _norm_mod(x_ref[...], mb_ref, ml_ref, 0, 1).astype(h_ref.dtype)


def _mixer_norm(x2, mod_base, mod_l, S, tm=256):
    N, D = x2.shape
    per_b = S // tm
    return pl.pallas_call(
        _norm_kernel,
        grid=(N // tm,),
        in_specs=[pl.BlockSpec((tm, D), lambda i: (i, 0)),
                  pl.BlockSpec((1, N_MOD, D), lambda i: (i // per_b, 0, 0)),
                  pl.BlockSpec((N_MOD, D), lambda i: (0, 0))],
        out_specs=pl.BlockSpec((tm, D), lambda i: (i, 0)),
        out_shape=jax.ShapeDtypeStruct((N, D), BF16),
        compiler_params=_params(("parallel",)),
        name="mixer_norm",
    )(x2, mod_base, mod_l)


def _gelu(x):
    return 0.5 * x * (1.0 + lax.erf(x * (1.0 / math.sqrt(2.0))))


def _proj_kernel(*refs, mode, sub):
    if mode == "gelu_ln":
        h_ref, w_ref, lg_ref, lb_ref, o_ref = refs
    else:
        h_ref, w_ref, o_ref = refs
    bn = o_ref.shape[1]
    for s in range(bn // sub):
        acc = jnp.dot(h_ref[...], w_ref[:, s * sub:(s + 1) * sub], preferred_element_type=F32)
        for hh in range(sub // HEAD_DIM):
            t = acc[:, hh * HEAD_DIM:(hh + 1) * HEAD_DIM]
            sl = slice(s * sub + hh * HEAD_DIM, s * sub + (hh + 1) * HEAD_DIM)
            if mode == "gelu":
                r = _gelu(t)
            elif mode == "gelu_ln":
                g = _gelu(t)
                gc = g - jnp.mean(g, axis=-1, keepdims=True)
                r = gc * lax.rsqrt(jnp.mean(gc * gc, axis=-1, keepdims=True) + EPS)
                r = r * lg_ref[:, sl] + lb_ref[:, sl]
            else:
                r = t
            o_ref[:, sl] = r.astype(o_ref.dtype)


def _proj(h, w, col0, width, mode, extra=(), extra_specs=(), bm=1024, bn=1024, sub=256):
    N, D = h.shape
    j0 = col0 // bn
    return pl.pallas_call(
        functools.partial(_proj_kernel, mode=mode, sub=sub),
        grid=(N // bm, width // bn),
        in_specs=[pl.BlockSpec((bm, D), lambda i, j: (i, 0)),
                  pl.BlockSpec((D, bn), lambda i, j: (0, j0 + j))] + list(extra_specs),
        out_specs=pl.BlockSpec((bm, bn), lambda i, j: (i, j)),
        out_shape=jax.ShapeDtypeStruct((N, width), BF16),
        compiler_params=_params(("parallel", "arbitrary")),
        name="in_proj_" + mode,
    )(h, w, *extra)


MXU_DIM = 256


def _deinterleave_gate_up(w2, g2, u2):
    r = lax.broadcasted_iota(I32, (MXU_DIM, MXU_DIM), 0)
    c = lax.broadcasted_iota(I32, (MXU_DIM, MXU_DIM), 1)
    src = jnp.where(c < LANES, 2 * c, 2 * (c - LANES) + 1)
    perm = jnp.where(r == src, 1.0, 0.0).astype(BF16)
    for ch in range(w2.shape[1] // MXU_DIM):
        w = w2[:, ch * MXU_DIM:(ch + 1) * MXU_DIM].astype(BF16)
        res = jnp.dot(w, perm, preferred_element_type=F32)
        g2[:, ch * LANES:(ch + 1) * LANES] = res[:, :LANES].astype(BF16)
        u2[:, ch * LANES:(ch + 1) * LANES] = res[:, LANES:].astype(BF16)


def _qk_proj_kernel(h_ref, w_ref, gain_ref, cos_ref, s1_ref, s2_ref, *rest):
    if len(rest) == 6:
        side_ref, o_ref, g_ref, u_ref, acc0, acc1 = rest
    else:
        side_ref = None
        o_ref, acc0, acc1 = rest
    t = pl.program_id(0)
    bn = o_ref.shape[1]

    @pl.when(t == 0)
    def _():
        acc1[...] = jnp.zeros_like(acc1)

    def step(acc_mm, acc_ep):
        acc_mm[...] = jnp.dot(h_ref[...], w_ref[...], preferred_element_type=F32)
        if side_ref is not None:
            _deinterleave_gate_up(side_ref.at[0], g_ref, u_ref)
        for hh in range(bn // HEAD_DIM):
            sl = slice(hh * HEAD_DIM, (hh + 1) * HEAD_DIM)
            a = acc_ep[:, sl]
            y = a * lax.rsqrt(jnp.mean(a * a, axis=-1, keepdims=True) + EPS) * gain_ref[0]
            r = (y * cos_ref[...] + pltpu.roll(y, HEAD_DIM - ROT_DIM // 2, 1) * s1_ref[...]
                 + pltpu.roll(y, ROT_DIM // 2, 1) * s2_ref[...])
            o_ref[:, sl] = r.astype(o_ref.dtype)

    @pl.when(t % 2 == 0)
    def _():
        step(acc0, acc1)

    @pl.when(t % 2 == 1)
    def _():
        step(acc1, acc0)


def _qk_proj(h, w, qk_gain, cos_t, s1_t, s2_t, side3, layer, bm=1024, bn=512):
    N, D = h.shape
    width = 2 * D_ATTN
    nj = width // bn
    n_tiles = (N // bm) * nj
    tiles_per_part = D_ATTN // bn
    mm = lambda t: jnp.minimum(t, n_tiles - 1)
    ep = lambda t: jnp.maximum(t - 1, 0)
    tok_tab = pl.BlockSpec((bm, HEAD_DIM), lambda t: (ep(t) // nj, 0))
    _, side_rows, side_cols = side3.shape
    rows = _rider_rows(side_rows, side_cols, n_tiles)
    in_specs = [pl.BlockSpec((bm, D), lambda t: (mm(t) // nj, 0)),
                pl.BlockSpec((D, bn), lambda t: (0, mm(t) % nj)),
                pl.BlockSpec((1, 1, HEAD_DIM), lambda t: ((ep(t) % nj) // tiles_per_part, 0, 0)),
                tok_tab, tok_tab, tok_tab]
    out_specs = [pl.BlockSpec((bm, bn), lambda t: (ep(t) // nj, ep(t) % nj))]
    out_shape = [jax.ShapeDtypeStruct((N, width), BF16)]
    args = [h, w, qk_gain, cos_t, s1_t, s2_t]
    if rows is not None:
        half_spec = pl.BlockSpec((rows, side_cols // 2), lambda t: (mm(t), 0))
        in_specs.append(pl.BlockSpec((1, rows, side_cols), lambda t: (layer, mm(t), 0)))
        out_specs += [half_spec, half_spec]
        out_shape += [jax.ShapeDtypeStruct((side_rows, side_cols // 2), BF16)] * 2
        args.append(side3)
    res = pl.pallas_call(
        _qk_proj_kernel,
        grid=(n_tiles + 1,),
        in_specs=in_specs,
        out_specs=out_specs,
        out_shape=out_shape,
        scratch_shapes=[pltpu.VMEM((bm, bn), F32), pltpu.VMEM((bm, bn), F32)],
        compiler_params=_params(("arbitrary",)),
        name="in_proj_qk",
    )(*args)
    return (res[0], res[1], res[2]) if rows is not None else (res[0], None, None)


def _cast_kernel(x_ref, o_ref):
    o_ref[...] = x_ref[0].astype(o_ref.dtype)


def _to_bf16(w3, layer, tr):
    _, R, C = w3.shape
    return pl.pallas_call(
        _cast_kernel,
        grid=(R // tr,),
        in_specs=[pl.BlockSpec((1, tr, C), lambda i: (layer, i, 0))],
        out_specs=pl.BlockSpec((tr, C), lambda i: (i, 0)),
        out_shape=jax.ShapeDtypeStruct((R, C), BF16),
        compiler_params=_params(("parallel",)),
        name="to_bf16",
    )(w3)


def _in_proj(h, w_in, qk_gain, cos_t, s1_t, s2_t, ln_g, ln_b, gate_up3, layer, bm=1024, bn=1024):
    qk, wg, wu = _qk_proj(h, w_in, qk_gain, cos_t, s1_t, s2_t, gate_up3, layer)
    v = _proj(h, w_in, 2 * D_ATTN, D_ATTN, "none", bm=bm, bn=bn)
    u = _proj(h, w_in, 3 * D_ATTN, D_GMLP, "gelu", bm=bm, bn=bn)
    row = pl.BlockSpec((1, bn), lambda i, j: (0, j))
    gv = _proj(h, w_in, 3 * D_ATTN + D_GMLP, D_GMLP, "gelu_ln", (ln_g, ln_b), (row, row), bm, bn)
    return qk, v, u, gv, wg, wu


def _attn_kernel(q_ref, k_ref, v_ref, bias_ref, o_ref, *, tq):
    S = q_ref.shape[1]
    n_q = S // tq
    for i in range(n_q):
        kend = (i + 1) * tq
        q = q_ref[0, i * tq:(i + 1) * tq, :]
        s = lax.dot_general(q, k_ref[0, :kend, :], (((1,), (1,)), ((), ())),
                            preferred_element_type=F32)
        off = (n_q - 1 - i) * tq
        s = s + bias_ref[:, off:off + kend]
        m = jnp.max(s, axis=-1, keepdims=True)
        p = jnp.exp(s - m)
        l = jnp.sum(p, axis=-1, keepdims=True)
        o = jnp.dot(p.astype(BF16), v_ref[0, :kend, :], preferred_element_type=F32)
        o_ref[0, i * tq:(i + 1) * tq, :] = (o / l).astype(o_ref.dtype)


def _attn_bias_table(S, tq):
    r = jnp.arange(tq, dtype=I32)[:, None]
    u = jnp.arange(S, dtype=I32)[None, :]
    d = r + (S - tq) - u
    mult = jnp.zeros((tq, S), F32)
    for window, dil in DILATED_PATTERNS:
        mult = mult + ((d >= 0) & (d % dil == 0) & (d <= window)).astype(F32)
    return jnp.where(mult > 0, jnp.log(jnp.maximum(mult, 1.0)), MASKED)


def _attention(qk3, v3, tq=256):
    B, S, _ = v3.shape
    bias = _attn_bias_table(S, tq)
    H = N_HEADS_ATTN
    return pl.pallas_call(
        functools.partial(_attn_kernel, tq=tq),
        grid=(B, H),
        in_specs=[pl.BlockSpec((1, S, HEAD_DIM), lambda b, h: (b, 0, h)),
                  pl.BlockSpec((1, S, HEAD_DIM), lambda b, h: (b, 0, H + h)),
                  pl.BlockSpec((1, S, HEAD_DIM), lambda b, h: (b, 0, h)),
                  pl.BlockSpec((tq, S), lambda b, h: (0, 0))],
        out_specs=pl.BlockSpec((1, S, HEAD_DIM), lambda b, h: (b, 0, h)),
        out_shape=jax.ShapeDtypeStruct((B, S, D_ATTN), BF16),
        compiler_params=_params(("parallel", "parallel")),
        name="dilated_attn",
    )(qk3, qk3, v3, bias)


def _gmlp_kernel(a_ref, u_ref, gv_ref, ws_ref, bst_ref, on_ref, o_ref, g_scr):
    tg = a_ref.shape[0]
    row = lax.broadcasted_iota(I32, (GMLP_CHUNK, GMLP_CHUNK), 0)
    col = lax.broadcasted_iota(I32, (GMLP_CHUNK, GMLP_CHUNK), 1)
    causal = row >= col
    for g in range(N_GROUPS_GMLP):
        w = jnp.where(causal, ws_ref[g], 0.0).astype(BF16)
        bias = bst_ref[:, g:g + 1]
        sl = slice(g * HEAD_DIM, (g + 1) * HEAD_DIM)
        for c in range(tg // GMLP_CHUNK):
            rows = slice(c * GMLP_CHUNK, (c + 1) * GMLP_CHUNK)
            mixed = jnp.dot(w, gv_ref[rows, sl], preferred_element_type=F32) + bias
            g_scr[rows, sl] = u_ref[rows, sl].astype(F32) * mixed
    gm = g_scr[...]
    gm = gm * lax.rsqrt(jnp.mean(gm * gm, axis=-1, keepdims=True) + EPS)
    o_ref[:, D_ATTN:] = (gm * on_ref[:, D_ATTN:]).astype(o_ref.dtype)
    a = a_ref[...].astype(F32)
    a = a * lax.rsqrt(jnp.mean(a * a, axis=-1, keepdims=True) + EPS)
    o_ref[:, :D_ATTN] = (a * on_ref[:, :D_ATTN]).astype(o_ref.dtype)


def _gmlp_and_norms(attn2, u, gv, w_s, b_s_t, out_norm, tg=512):
    N = attn2.shape[0]
    D_mix = D_ATTN + D_GMLP
    return pl.pallas_call(
        _gmlp_kernel,
        grid=(N // tg,),
        in_specs=[pl.BlockSpec((tg, D_ATTN), lambda i: (i, 0)),
                  pl.BlockSpec((tg, D_GMLP), lambda i: (i, 0)),
                  pl.BlockSpec((tg, D_GMLP), lambda i: (i, 0)),
                  pl.BlockSpec((N_GROUPS_GMLP, GMLP_CHUNK, GMLP_CHUNK), lambda i: (0, 0, 0)),
                  pl.BlockSpec((GMLP_CHUNK, N_GROUPS_GMLP), lambda i: (0, 0)),
                  pl.BlockSpec((1, D_mix), lambda i: (0, 0))],
        out_specs=pl.BlockSpec((tg, D_mix), lambda i: (i, 0)),
        out_shape=jax.ShapeDtypeStruct((N, D_mix), BF16),
        scratch_shapes=[pltpu.VMEM((tg, D_GMLP), F32)],
        compiler_params=_params(("parallel",)),
        name="gmlp_norms",
    )(attn2, u, gv, w_s, b_s_t, out_norm)


RIDER_MAX_BYTES = 8 * 1024 * 1024


def _rider_rows(total_rows, cols, steps):
    if total_rows % steps:
        return None
    rows = total_rows // steps
    if rows % 16 or rows * cols * 4 > RIDER_MAX_BYTES:
        return None
    return rows


def _outproj_kernel(m_ref, w_ref, x_ref, mb_ref, ml_ref, *rest):
    if len(rest) == 3:
        side_ref, o_ref, side_out = rest
        side_out[...] = side_ref[0].astype(side_out.dtype)
    else:
        o_ref, = rest
    acc = jnp.dot(m_ref[...], w_ref[...], preferred_element_type=F32)
    o_ref[...] = x_ref[...] + _mod_row(mb_ref, ml_ref, 2) * acc


def _out_proj(mix, w_out, x2, mod_base, mod_l, S, side3, layer, bm=1024, bn=512):
    N, D_mix = mix.shape
    D = w_out.shape[1]
    per_b = S // bm
    nj = D // bn
    _, side_rows, side_cols = side3.shape
    rows = _rider_rows(side_rows, side_cols, (N // bm) * nj)
    in_specs = [pl.BlockSpec((bm, D_mix), lambda i, j: (i, 0)),
                pl.BlockSpec((D_mix, bn), lambda i, j: (0, j)),
                pl.BlockSpec((bm, bn), lambda i, j: (i, j)),
                pl.BlockSpec((1, N_MOD, bn), lambda i, j: (i // per_b, 0, j)),
                pl.BlockSpec((N_MOD, bn), lambda i, j: (0, j))]
    out_specs = [pl.BlockSpec((bm, bn), lambda i, j: (i, j))]
    out_shape = [jax.ShapeDtypeStruct((N, D), F32)]
    args = [mix, w_out, x2, mod_base, mod_l]
    if rows is not None:
        in_specs.append(pl.BlockSpec((1, rows, side_cols), lambda i, j: (layer, i * nj + j, 0)))
        out_specs.append(pl.BlockSpec((rows, side_cols), lambda i, j: (i * nj + j, 0)))
        out_shape.append(jax.ShapeDtypeStruct((side_rows, side_cols), BF16))
        args.append(side3)
    res = pl.pallas_call(
        _outproj_kernel,
        grid=(N // bm, nj),
        in_specs=in_specs,
        out_specs=out_specs,
        out_shape=out_shape,
        compiler_params=_params(("parallel", "arbitrary")),
        name="out_proj",
    )(*args)
    return (res[0], res[1]) if rows is not None else (res[0], None)


def _router_kernel(x_ref, mb_ref, ml_ref, wr_ref, br_ref,
                   h_ref, idx_ref, gate_ref, rank_ref, cnt_ref, cnt_scr):
    i = pl.program_id(0)
    tm = x_ref.shape[0]

    @pl.when(i == 0)
    def _():
        cnt_scr[...] = jnp.zeros_like(cnt_scr)

    h = _norm_mod(x_ref[...], mb_ref, ml_ref, 3, 4)
    h_ref[...] = _pack_halves(h)
    logits = jnp.dot(h.astype(BF16), wr_ref[...].astype(BF16),
                     preferred_element_type=F32) + br_ref[...]
    lane = lax.broadcasted_iota(I32, (tm, LANES), 1)
    lane_f = lane.astype(F32)
    vals = jnp.where(lane < N_EXPERTS, logits, -jnp.inf)
    tops, sels = [], []
    idx_out = jnp.zeros((tm, LANES), F32)
    for k in range(TOP_K):
        m = jnp.max(vals, axis=-1, keepdims=True)
        first = jnp.min(jnp.where(vals == m, lane_f, float(LANES)), axis=-1, keepdims=True)
        sel = lane_f == first
        tops.append(m)
        sels.append(sel)
        idx_out = jnp.where(lane == k, first, idx_out)
        vals = jnp.where(sel, -jnp.inf, vals)
    exps = [jnp.exp(t - tops[0]) for t in tops]
    denom = exps[0] + exps[1] + exps[2] + exps[3]
    gate_out = jnp.zeros((tm, LANES), F32)
    for k in range(TOP_K):
        gate_out = jnp.where(lane == k, exps[k] / denom, gate_out)

    chosen = jnp.zeros((tm, LANES), F32)
    for sel in sels:
        chosen = chosen + sel.astype(F32)
    r = lax.broadcasted_iota(I32, (tm, tm), 0)
    c = lax.broadcasted_iota(I32, (tm, tm), 1)
    earlier = jnp.where(r > c, 1.0, 0.0).astype(BF16)
    before = jnp.dot(earlier, chosen.astype(BF16), preferred_element_type=F32) + cnt_scr[...]
    rank_out = jnp.zeros((tm, LANES), F32)
    for k in range(TOP_K):
        rk = jnp.sum(jnp.where(sels[k], before, 0.0), axis=-1, keepdims=True)
        rank_out = jnp.where(lane == k, rk, rank_out)
    cnt_scr[...] = cnt_scr[...] + jnp.sum(chosen, axis=0, keepdims=True)

    idx_ref[...] = idx_out.astype(I32)
    gate_ref[...] = gate_out
    rank_ref[...] = rank_out.astype(I32)
    cnt_ref[...] = cnt_scr[...].astype(I32)


def _ffn_norm_router(x2, mod_base, mod_l, w_router_p, b_router_p, S, tm=256):
    N, D = x2.shape
    per_b = S // tm
    tok_lane = pl.BlockSpec((tm, LANES), lambda i: (i, 0))
    return pl.pallas_call(
        _router_kernel,
        grid=(N // tm,),
        in_specs=[pl.BlockSpec((tm, D), lambda i: (i, 0)),
                  pl.BlockSpec((1, N_MOD, D), lambda i: (i // per_b, 0, 0)),
                  pl.BlockSpec((N_MOD, D), lambda i: (0, 0)),
                  pl.BlockSpec((D, LANES), lambda i: (0, 0)),
                  pl.BlockSpec((1, LANES), lambda i: (0, 0))],
        out_specs=[pl.BlockSpec((tm, D // 2), lambda i: (i, 0)), tok_lane, tok_lane, tok_lane,
                   pl.BlockSpec((1, LANES), lambda i: (0, 0))],
        out_shape=[jax.ShapeDtypeStruct((N, D // 2), jnp.uint32),
                   jax.ShapeDtypeStruct((N, LANES), I32),
                   jax.ShapeDtypeStruct((N, LANES), F32),
                   jax.ShapeDtypeStruct((N, LANES), I32),
                   jax.ShapeDtypeStruct((1, LANES), I32)],
        scratch_shapes=[pltpu.VMEM((1, LANES), F32)],
        compiler_params=_params(("arbitrary",)),
        name="ffn_norm_router",
    )(x2, mod_base, mod_l, w_router_p, b_router_p)


def _split_gate_up_kernel(w_ref, g_ref, u_ref):
    _deinterleave_gate_up(w_ref.at[0, 0], g_ref.at[0], u_ref.at[0])


def _split_gate_up(w_gate_up, layer, tr=1024):
    _, E, D, two_ff = w_gate_up.shape
    d_ff = two_ff // 2
    out = jax.ShapeDtypeStruct((E, D, d_ff), BF16)
    return pl.pallas_call(
        _split_gate_up_kernel,
        grid=(E, D // tr),
        in_specs=[pl.BlockSpec((1, 1, tr, two_ff), lambda e, i: (layer, e, i, 0))],
        out_specs=[pl.BlockSpec((1, tr, d_ff), lambda e, i: (e, i, 0)),
                   pl.BlockSpec((1, tr, d_ff), lambda e, i: (e, i, 0))],
        out_shape=[out, out],
        compiler_params=_params(("parallel", "parallel")),
        name="split_gate_up",
    )(w_gate_up)


def _dispatch_kernel(dest_ref, pad_start_ref, pad_len_ref, nused_ref, h_ref, xs_hbm,
                     zero_buf, sem, pad_sem, blk_sem, *, td, tm):
    i = pl.program_id(0)
    n = pl.num_programs(0)
    n_blocks = xs_hbm.shape[0] // tm

    def body(r, carry):
        base = (i * td + r) * TOP_K
        for k in range(TOP_K):
            pltpu.make_async_copy(h_ref.at[pl.ds(r, 1), :], xs_hbm.at[pl.ds(dest_ref[base + k], 1), :],
                                  sem).start()
        return carry
    lax.fori_loop(0, td, body, 0, unroll=2)

    for k in range(TOP_K):
        pltpu.make_async_copy(h_ref, xs_hbm.at[pl.ds(0, td), :], sem).wait()

    @pl.when(i == n - 1)
    def _():
        zero_buf[...] = jnp.zeros_like(zero_buf)

        def per_expert(e, total):
            start = pad_start_ref[e]
            count = pad_len_ref[e]

            def fill(r, carry):
                pltpu.make_async_copy(zero_buf.at[pl.ds(0, 1), :], xs_hbm.at[pl.ds(start + r, 1), :],
                                      pad_sem).start()
                return carry
            lax.fori_loop(0, count, fill, 0)
            return total + count
        total = lax.fori_loop(0, N_EXPERTS, per_expert, 0)

        def block_copy(blk):
            return pltpu.make_async_copy(zero_buf, xs_hbm.at[pl.ds(pl.multiple_of(blk * tm, tm), tm), :],
                                         blk_sem)

        def fill_block(blk, carry):
            block_copy(blk).start()
            return carry
        lax.fori_loop(nused_ref[0], n_blocks, fill_block, 0)

        def retire(r, carry):
            pltpu.make_async_copy(zero_buf.at[pl.ds(0, 1), :], xs_hbm.at[pl.ds(0, 1), :], pad_sem).wait()
            return carry
        lax.fori_loop(0, total, retire, 0)

        def retire_block(blk, carry):
            block_copy(blk).wait()
            return carry
        lax.fori_loop(nused_ref[0], n_blocks, retire_block, 0)


def _dispatch(dest_flat, pad_start, pad_len, nused, h2, n_slots, tm, td=1024):
    N, half = h2.shape
    grid_spec = pltpu.PrefetchScalarGridSpec(
        num_scalar_prefetch=4,
        grid=(N // td,),
        in_specs=[pl.BlockSpec((td, half), lambda i, d, ps, pn, nu: (i, 0))],
        out_specs=pl.BlockSpec(memory_space=pl.ANY),
        scratch_shapes=[pltpu.VMEM((tm, half), jnp.uint32), pltpu.SemaphoreType.DMA(()),
                        pltpu.SemaphoreType.DMA(()), pltpu.SemaphoreType.DMA(())],
    )
    return pl.pallas_call(
        functools.partial(_dispatch_kernel, td=td, tm=tm),
        grid_spec=grid_spec,
        out_shape=jax.ShapeDtypeStruct((n_slots, half), jnp.uint32),
        compiler_params=_params(("arbitrary",)),
        name="moe_dispatch",
    )(dest_flat, pad_start, pad_len, nused, h2)


def _expert_kernel(be_ref, nused_ref, x_ref, wg_ref, wu_ref, wd_ref, bg_ref, bu_ref, bd_ref, y_ref):
    b = pl.program_id(0)

    @pl.when(b < nused_ref[0])
    def _():
        lo, hi = _unpack_halves(x_ref[...])
        x = jnp.concatenate([lo.astype(BF16), hi.astype(BF16)], axis=1)
        gate = jnp.dot(x, wg_ref[0], preferred_element_type=F32) + bg_ref[0]
        up = jnp.dot(x, wu_ref[0], preferred_element_type=F32) + bu_ref[0]
        gate = jnp.minimum(gate, SWIGLU_LIMIT)
        up = jnp.clip(up, -SWIGLU_LIMIT, SWIGLU_LIMIT)
        act = (up + 1.0) * gate * jax.nn.sigmoid(SWIGLU_ALPHA * gate)
        y = jnp.dot(act.astype(BF16), wd_ref[0], preferred_element_type=F32) + bd_ref[0]
        y_ref[...] = _pack_halves(y)

    @pl.when(b >= nused_ref[0])
    def _():
        y_ref[...] = jnp.zeros_like(y_ref)


def _experts(xs, block_e, nused, wg, wu, wd, bg, bu, bd, tm):
    half = xs.shape[1]
    D = 2 * half
    n_blocks = block_e.shape[0]
    d_ff = wg.shape[2]
    e_map = lambda b, be, nu: (be[b], 0, 0)
    x_map = lambda b, be, nu: (jnp.minimum(b, nu[0] - 1), 0)
    grid_spec = pltpu.PrefetchScalarGridSpec(
        num_scalar_prefetch=2,
        grid=(n_blocks,),
        in_specs=[pl.BlockSpec((tm, half), x_map),
                  pl.BlockSpec((1, D, d_ff), e_map),
                  pl.BlockSpec((1, D, d_ff), e_map),
                  pl.BlockSpec((1, d_ff, D), e_map),
                  pl.BlockSpec((1, 1, d_ff), e_map),
                  pl.BlockSpec((1, 1, d_ff), e_map),
                  pl.BlockSpec((1, 1, D), e_map)],
        out_specs=pl.BlockSpec((tm, half), lambda b, be, nu: (b, 0)),
    )
    return pl.pallas_call(
        _expert_kernel,
        grid_spec=grid_spec,
        out_shape=jax.ShapeDtypeStruct((n_blocks * tm, half), jnp.uint32),
        compiler_params=_params(("arbitrary",)),
        name="experts",
    )(block_e, nused, xs, wg, wu, wd, bg, bu, bd)


SUBLANES = 8


def _combine_kernel(dest_ref, y_hbm, x_ref, gate_ref, mb_ref, ml_ref, o_ref, buf0, buf1, sem, *, tc):
    i = pl.program_id(0)
    n = pl.num_programs(0)
    bufs = (buf0, buf1)
    half = buf0.shape[2]

    def row_copy(d, slot, k, r):
        return pltpu.make_async_copy(y_hbm.at[pl.ds(d, 1), :],
                                     bufs[slot].at[k, pl.ds(r, 1), :], sem.at[slot])

    def issue_rows(step, slot, r0, count):
        for rr in range(count):
            base = (step * tc + r0 + rr) * TOP_K
            for k in range(TOP_K):
                row_copy(dest_ref[base + k], slot, k, r0 + rr).start()

    def wait_step(slot):
        for k in range(TOP_K):
            pltpu.make_async_copy(y_hbm.at[pl.ds(0, tc), :], bufs[slot].at[k], sem.at[slot]).wait()

    @pl.when(i == 0)
    def _():
        def body(c, carry):
            issue_rows(0, 0, c * SUBLANES, SUBLANES)
            return carry
        lax.fori_loop(0, tc // SUBLANES, body, 0)

    nxt = jnp.minimum(i + 1, n - 1)
    gate_f = _mod_row(mb_ref, ml_ref, 5)

    def step(slot):
        wait_step(slot)

        def body(c, carry):
            r0 = pl.multiple_of(c * SUBLANES, SUBLANES)
            issue_rows(nxt, 1 - slot, r0, SUBLANES)
            rows = pl.ds(r0, SUBLANES)
            acc_lo = jnp.zeros((SUBLANES, half), F32)
            acc_hi = jnp.zeros((SUBLANES, half), F32)
            for k in range(TOP_K):
                lo, hi = _unpack_halves(bufs[slot][k, rows, :])
                g = gate_ref[rows, k:k + 1]
                acc_lo = acc_lo + g * lo
                acc_hi = acc_hi + g * hi
            o_ref[rows, :half] = x_ref[rows, :half] + gate_f[:, :half] * acc_lo
            o_ref[rows, half:] = x_ref[rows, half:] + gate_f[:, half:] * acc_hi
            return carry
        lax.fori_loop(0, tc // SUBLANES, body, 0)

        @pl.when(i == n - 1)
        def _():
            wait_step(1 - slot)

    for slot in range(2):
        @pl.when(i % 2 == slot)
        def _(slot=slot):
            step(slot)


def _combine(dest_flat, y, x2, gates, mod_base, mod_l, S, tc=256):
    N, D = x2.shape
    per_b = S // tc
    grid_spec = pltpu.PrefetchScalarGridSpec(
        num_scalar_prefetch=1,
        grid=(N // tc,),
        in_specs=[pl.BlockSpec(memory_space=pl.ANY),
                  pl.BlockSpec((tc, D), lambda i, d: (i, 0)),
                  pl.BlockSpec((tc, LANES), lambda i, d: (i, 0)),
                  pl.BlockSpec((1, N_MOD, D), lambda i, d: (i // per_b, 0, 0)),
                  pl.BlockSpec((N_MOD, D), lambda i, d: (0, 0))],
        out_specs=pl.BlockSpec((tc, D), lambda i, d: (i, 0)),
        scratch_shapes=[pltpu.VMEM((TOP_K, tc, D // 2), jnp.uint32),
                        pltpu.VMEM((TOP_K, tc, D // 2), jnp.uint32), pltpu.SemaphoreType.DMA((2,))],
    )
    return pl.pallas_call(
        functools.partial(_combine_kernel, tc=tc),
        grid_spec=grid_spec,
        out_shape=jax.ShapeDtypeStruct((N, D), F32),
        compiler_params=_params(("arbitrary",)),
        name="moe_combine",
    )(dest_flat, y, x2, gates, mod_base, mod_l)


EXPERT_ROWS = 256


def _rope_tables(positions):
    half = ROT_DIM // 2
    inv_freq = ROPE_THETA ** (-jnp.arange(half, dtype=F32) * 2.0 / ROT_DIM)
    ang = positions.astype(F32).reshape(-1, 1) * inv_freq
    cos, sin = jnp.cos(ang), jnp.sin(ang)
    n = ang.shape[0]
    ones = jnp.ones((n, HEAD_DIM - ROT_DIM), F32)
    zeros = jnp.zeros((n, HEAD_DIM - half), F32)
    cos_t = jnp.concatenate([cos, cos, ones], axis=1)
    s1_t = jnp.concatenate([-sin, zeros], axis=1)
    s2_t = jnp.concatenate([jnp.zeros((n, half), F32), sin,
                            jnp.zeros((n, HEAD_DIM - ROT_DIM), F32)], axis=1)
    return cos_t, s1_t, s2_t


def _routing_tables(top_idx, rank, counts, tm, n_blocks):
    N = top_idx.shape[0]
    nblk = (counts + tm - 1) // tm
    blk_end = jnp.cumsum(nblk)
    blk_start = blk_end - nblk
    dest = blk_start[top_idx] * tm + rank
    owner = jnp.sum((blk_end[None, :] <= jnp.arange(n_blocks, dtype=I32)[:, None]).astype(I32), axis=1)
    block_e = jnp.minimum(owner, N_EXPERTS - 1).astype(I32)
    nused = blk_end[-1:].astype(I32)
    pad_start = (blk_start * tm + counts).astype(I32)
    pad_len = (nblk * tm - counts).astype(I32)
    return dest.reshape(-1).astype(I32), block_e, nused, pad_start, pad_len


def kernel(x, c, positions, w_mod, b_mod, mod_layer, w_in, q_norm, k_norm, w_s, b_s, v_ln_g, v_ln_b,
           out_norm, w_out, w_router, b_router, w_gate_up, b_gate_up, w_down, b_down):
    B, S, D = x.shape
    N = B * S
    depth = w_in.shape[0]
    d_ff = w_down.shape[2]
    tm_e = EXPERT_ROWS
    n_blocks = N * TOP_K // tm_e + N_EXPERTS

    mod_base = _mod_base(c, w_mod, b_mod).reshape(B, N_MOD, D)
    cos_t, s1_t, s2_t = _rope_tables(positions)
    x2 = x.reshape(N, D)

    for l in range(depth):
        mod_l = mod_layer[l]
        h = _mixer_norm(x2, mod_base, mod_l, S)
        qk_gain = jnp.stack([q_norm[l] * HEAD_DIM ** -0.5, k_norm[l]]).reshape(2, 1, HEAD_DIM)
        qk, v, u, gv, wg, wu = _in_proj(h, _to_bf16(w_in, l, 256), qk_gain, cos_t, s1_t, s2_t,
                                        v_ln_g[l].reshape(1, D_GMLP), v_ln_b[l].reshape(1, D_GMLP),
                                        w_gate_up.reshape(depth, N_EXPERTS * D, 2 * d_ff), l)
        attn = _attention(qk.reshape(B, S, -1), v.reshape(B, S, -1))
        mix = _gmlp_and_norms(attn.reshape(N, D_ATTN), u, gv, w_s[l], jnp.transpose(b_s[l]),
                              out_norm[l].reshape(1, -1))
        w_down3 = w_down.reshape(depth, N_EXPERTS * d_ff, D)
        x2, wd = _out_proj(mix, _to_bf16(w_out, l, 512), x2, mod_base, mod_l, S, w_down3, l)

        w_router_p = jnp.pad(w_router[l], ((0, 0), (0, LANES - N_EXPERTS)))
        b_router_p = jnp.pad(b_router[l], (0, LANES - N_EXPERTS)).reshape(1, LANES)
        h2, top_idx, gates, rank, counts = _ffn_norm_router(x2, mod_base, mod_l, w_router_p,
                                                            b_router_p, S)
        dest, block_e, nused, pad_start, pad_len = _routing_tables(
            top_idx[:, :TOP_K], rank[:, :TOP_K], counts[0, :N_EXPERTS], tm_e, n_blocks)
        xs = _dispatch(dest, pad_start, pad_len, nused, h2, n_blocks * tm_e, tm_e)
        if wg is None:
            wg, wu = _split_gate_up(w_gate_up, l)
        if wd is None:
            wd = _to_bf16(w_down3, l, 512)
        wg = wg.reshape(N_EXPERTS, D, d_ff)
        wu = wu.reshape(N_EXPERTS, D, d_ff)
        wd = wd.reshape(N_EXPERTS, d_ff, D)
        bg = b_gate_up[l][:, 0::2].reshape(N_EXPERTS, 1, d_ff)
        bu = b_gate_up[l][:, 1::2].reshape(N_EXPERTS, 1, d_ff)
        y = _experts(xs, block_e, nused, wg, wu, wd, bg, bu,
                     b_down[l].reshape(N_EXPERTS, 1, D), tm_e)
        x2 = _combine(dest, y, x2, gates, mod_base, mod_l, S)
    return x2.reshape(B, S, D)
```

```python
import functools
import math

import jax
import jax.numpy as jnp
from jax import lax
from jax.experimental import pallas as pl
from jax.experimental.pallas import tpu as pltpu

F32 = jnp.float32
BF16 = jnp.bfloat16
I32 = jnp.int32

HEAD_DIM = 128
N_HEADS_ATTN = 16
N_GROUPS_GMLP = 16
D_ATTN = N_HEADS_ATTN * HEAD_DIM
D_GMLP = N_GROUPS_GMLP * HEAD_DIM
ROT_DIM = HEAD_DIM // 4
ROPE_THETA = 500000.0
DILATED_PATTERNS = ((128, 1), (512, 4), (2048, 16))
GMLP_CHUNK = 128
N_EXPERTS = 32
TOP_K = 4
SWIGLU_LIMIT = 7.0
SWIGLU_ALPHA = 1.702
N_MOD = 6
EPS = 1e-6
LANES = 128
MASKED = -1e30

VMEM_LIMIT = 56 * 1024 * 1024


def _params(sem, vmem=VMEM_LIMIT):
    return pltpu.CompilerParams(dimension_semantics=sem, vmem_limit_bytes=vmem)


def _mod_kernel(c_ref, w_ref, b_ref, o_ref):
    c = c_ref[...]
    s = c * jax.nn.sigmoid(c)
    o_ref[...] = jnp.dot(s.astype(BF16), w_ref[...].astype(BF16),
                         preferred_element_type=F32) + b_ref[...]


def _mod_base(c, w_mod, b_mod, tn=512):
    B, D = c.shape
    n_out = w_mod.shape[1]
    return pl.pallas_call(
        _mod_kernel,
        grid=(n_out // tn,),
        in_specs=[pl.BlockSpec((B, D), lambda j: (0, 0)),
                  pl.BlockSpec((D, tn), lambda j: (0, j)),
                  pl.BlockSpec((1, tn), lambda j: (0, j))],
        out_specs=pl.BlockSpec((B, tn), lambda j: (0, j)),
        out_shape=jax.ShapeDtypeStruct((B, n_out), F32),
        compiler_params=_params(("parallel",)),
        name="mod_base",
    )(c, w_mod, b_mod.reshape(1, n_out))


def _pack_halves(x):
    c = x.shape[1] // 2
    bits = pltpu.bitcast(x.astype(BF16).astype(F32), jnp.uint32)
    return (bits[:, c:] & jnp.uint32(0xFFFF0000)) | (bits[:, :c] >> 16)


def _unpack_halves(p):
    lo = pltpu.bitcast(p << 16, F32)
    hi = pltpu.bitcast(p & jnp.uint32(0xFFFF0000), F32)
    return lo, hi


def _mod_row(mb_ref, ml_ref, idx):
    return mb_ref[0, idx:idx + 1, :] + ml_ref[idx:idx + 1, :]


def _norm_mod(x, mb_ref, ml_ref, shift_idx, scale_idx):
    y = x * lax.rsqrt(jnp.mean(x * x, axis=-1, keepdims=True) + EPS)
    return y * (1.0 + _mod_row(mb_ref, ml_ref, scale_idx)) + _mod_row(mb_ref, ml_ref, shift_idx)


def _norm_kernel(x_ref, mb_ref, ml_ref, h_ref):
    h_ref[...] = _norm_mod(x_ref[...], mb_ref, ml_ref, 0, 1).astype(h_ref.dtype)


def _mixer_norm(x2, mod_base, mod_l, S, tm=256):
    N, D = x2.shape
    per_b = S // tm
    return pl.pallas_call(
        _norm_kernel,
        grid=(N // tm,),
        in_specs=[pl.BlockSpec((tm, D), lambda i: (i, 0)),
                  pl.BlockSpec((1, N_MOD, D), lambda i: (i // per_b, 0, 0)),
                  pl.BlockSpec((N_MOD, D), lambda i: (0, 0))],
        out_specs=pl.BlockSpec((tm, D), lambda i: (i, 0)),
        out_shape=jax.ShapeDtypeStruct((N, D), BF16),
        compiler_params=_params(("parallel",)),
        name="mixer_norm",
    )(x2, mod_base, mod_l)


def _gelu(x):
    return 0.5 * x * (1.0 + lax.erf(x * (1.0 / math.sqrt(2.0))))


def _proj_kernel(*refs, mode, sub):
    if mode == "gelu_ln":
        h_ref, w_ref, lg_ref, lb_ref, o_ref = refs
    else:
        h_ref, w_ref, o_ref = refs
    bn = o_ref.shape[1]
    for s in range(bn // sub):
        acc = jnp.dot(h_ref[...], w_ref[:, s * sub:(s + 1) * sub], preferred_element_type=F32)
        for hh in range(sub // HEAD_DIM):
            t = acc[:, hh * HEAD_DIM:(hh + 1) * HEAD_DIM]
            sl = slice(s * sub + hh * HEAD_DIM, s * sub + (hh + 1) * HEAD_DIM)
            if mode == "gelu":
                r = _gelu(t)
            elif mode == "gelu_ln":
                g = _gelu(t)
                gc = g - jnp.mean(g, axis=-1, keepdims=True)
                r = gc * lax.rsqrt(jnp.mean(gc * gc, axis=-1, keepdims=True) + EPS)
                r = r * lg_ref[:, sl] + lb_ref[:, sl]
            else:
                r = t
            o_ref[:, sl] = r.astype(o_ref.dtype)


def _proj(h, w, col0, width, mode, extra=(), extra_specs=(), bm=1024, bn=1024, sub=256):
    N, D = h.shape
    j0 = col0 // bn
    return pl.pallas_call(
        functools.partial(_proj_kernel, mode=mode, sub=sub),
        grid=(N // bm, width // bn),
        in_specs=[pl.BlockSpec((bm, D), lambda i, j: (i, 0)),
                  pl.BlockSpec((D, bn), lambda i, j: (0, j0 + j))] + list(extra_specs),
        out_specs=pl.BlockSpec((bm, bn), lambda i, j: (i, j)),
        out_shape=jax.ShapeDtypeStruct((N, width), BF16),
        compiler_params=_params(("parallel", "arbitrary")),
        name="in_proj_" + mode,
    )(h, w, *extra)


MXU_DIM = 256


def _deinterleave_gate_up(w2, g2, u2):
    r = lax.broadcasted_iota(I32, (MXU_DIM, MXU_DIM), 0)
    c = lax.broadcasted_iota(I32, (MXU_DIM, MXU_DIM), 1)
    src = jnp.where(c < LANES, 2 * c, 2 * (c - LANES) + 1)
    perm = jnp.where(r == src, 1.0, 0.0).astype(BF16)
    for ch in range(w2.shape[1] // MXU_DIM):
        w = w2[:, ch * MXU_DIM:(ch + 1) * MXU_DIM].astype(BF16)
        res = jnp.dot(w, perm, preferred_element_type=F32)
        g2[:, ch * LANES:(ch + 1) * LANES] = res[:, :LANES].astype(BF16)
        u2[:, ch * LANES:(ch + 1) * LANES] = res[:, LANES:].astype(BF16)


def _qk_proj_kernel(h_ref, w_ref, gain_ref, cos_ref, s1_ref, s2_ref, *rest):
    if len(rest) == 6:
        side_ref, o_ref, g_ref, u_ref, acc0, acc1 = rest
    else:
        side_ref = None
        o_ref, acc0, acc1 = rest
    t = pl.program_id(0)
    bn = o_ref.shape[1]

    @pl.when(t == 0)
    def _():
        acc1[...] = jnp.zeros_like(acc1)

    def step(acc_mm, acc_ep):
        acc_mm[...] = jnp.dot(h_ref[...], w_ref[...], preferred_element_type=F32)
        if side_ref is not None:
            _deinterleave_gate_up(side_ref.at[0], g_ref, u_ref)
        for hh in range(bn // HEAD_DIM):
            sl = slice(hh * HEAD_DIM, (hh + 1) * HEAD_DIM)
            a = acc_ep[:, sl]
            y = a * lax.rsqrt(jnp.mean(a * a, axis=-1, keepdims=True) + EPS) * gain_ref[0]
            r = (y * cos_ref[...] + pltpu.roll(y, HEAD_DIM - ROT_DIM // 2, 1) * s1_ref[...]
                 + pltpu.roll(y, ROT_DIM // 2, 1) * s2_ref[...])
            o_ref[:, sl] = r.astype(o_ref.dtype)

    @pl.when(t % 2 == 0)
    def _():
        step(acc0, acc1)

    @pl.when(t % 2 == 1)
    def _():
        step(acc1, acc0)


def _qk_proj(h, w, qk_gain, cos_t, s1_t, s2_t, side3, layer, bm=1024, bn=512):
    N, D = h.shape
    width = 2 * D_ATTN
    nj = width // bn
    n_tiles = (N // bm) * nj
    tiles_per_part = D_ATTN // bn
    mm = lambda t: jnp.minimum(t, n_tiles - 1)
    ep = lambda t: jnp.maximum(t - 1, 0)
    tok_tab = pl.BlockSpec((bm, HEAD_DIM), lambda t: (ep(t) // nj, 0))
    _, side_rows, side_cols = side3.shape
    rows = _rider_rows(side_rows, side_cols, n_tiles)
    in_specs = [pl.BlockSpec((bm, D), lambda t: (mm(t) // nj, 0)),
                pl.BlockSpec((D, bn), lambda t: (0, mm(t) % nj)),
                pl.BlockSpec((1, 1, HEAD_DIM), lambda t: ((ep(t) % nj) // tiles_per_part, 0, 0)),
                tok_tab, tok_tab, tok_tab]
    out_specs = [pl.BlockSpec((bm, bn), lambda t: (ep(t) // nj, ep(t) % nj))]
    out_shape = [jax.ShapeDtypeStruct((N, width), BF16)]
    args = [h, w, qk_gain, cos_t, s1_t, s2_t]
    if rows is not None:
        half_spec = pl.BlockSpec((rows, side_cols // 2), lambda t: (mm(t), 0))
        in_specs.append(pl.BlockSpec((1, rows, side_cols), lambda t: (layer, mm(t), 0)))
        out_specs += [half_spec, half_spec]
        out_shape += [jax.ShapeDtypeStruct((side_rows, side_cols // 2), BF16)] * 2
        args.append(side3)
    res = pl.pallas_call(
        _qk_proj_kernel,
        grid=(n_tiles + 1,),
        in_specs=in_specs,
        out_specs=out_specs,
        out_shape=out_shape,
        scratch_shapes=[pltpu.VMEM((bm, bn), F32), pltpu.VMEM((bm, bn), F32)],
        compiler_params=_params(("arbitrary",)),
        name="in_proj_qk",
    )(*args)
    return (res[0], res[1], res[2]) if rows is not None else (res[0], None, None)


def _cast_kernel(x_ref, o_ref):
    o_ref[...] = x_ref[0].astype(o_ref.dtype)


def _to_bf16(w3, layer, tr):
    _, R, C = w3.shape
    return pl.pallas_call(
        _cast_kernel,
        grid=(R // tr,),
        in_specs=[pl.BlockSpec((1, tr, C), lambda i: (layer, i, 0))],
        out_specs=pl.BlockSpec((tr, C), lambda i: (i, 0)),
        out_shape=jax.ShapeDtypeStruct((R, C), BF16),
        compiler_params=_params(("parallel",)),
        name="to_bf16",
    )(w3)


def _in_proj(h, w_in, qk_gain, cos_t, s1_t, s2_t, ln_g, ln_b, gate_up3, layer, bm=1024, bn=1024):
    qk, wg, wu = _qk_proj(h, w_in, qk_gain, cos_t, s1_t, s2_t, gate_up3, layer)
    v = _proj(h, w_in, 2 * D_ATTN, D_ATTN, "none", bm=bm, bn=bn)
    u = _proj(h, w_in, 3 * D_ATTN, D_GMLP, "gelu", bm=bm, bn=bn)
    row = pl.BlockSpec((1, bn), lambda i, j: (0, j))
    gv = _proj(h, w_in, 3 * D_ATTN + D_GMLP, D_GMLP, "gelu_ln", (ln_g, ln_b), (row, row), bm, bn)
    return qk, v, u, gv, wg, wu


def _attn_kernel(q_ref, k_ref, v_ref, bias_ref, o_ref, *, tq):
    S = q_ref.shape[1]
    n_q = S // tq
    for i in range(n_q):
        kend = (i + 1) * tq
        off = (n_q - 1 - i) * tq
        for hh in range(q_ref.shape[2] // HEAD_DIM):
            cols = slice(hh * HEAD_DIM, (hh + 1) * HEAD_DIM)
            q = q_ref[0, i * tq:(i + 1) * tq, cols]
            s = lax.dot_general(q, k_ref[0, :kend, cols], (((1,), (1,)), ((), ())),
                                preferred_element_type=F32)
            s = s + bias_ref[:, off:off + kend]
            m = jnp.max(s, axis=-1, keepdims=True)
            p = jnp.exp(s - m)
            l = jnp.sum(p, axis=-1, keepdims=True)
            o = jnp.dot(p.astype(BF16), v_ref[0, :kend, cols], preferred_element_type=F32)
            o_ref[0, i * tq:(i + 1) * tq, cols] = (o / l).astype(o_ref.dtype)


def _attn_bias_table(S, tq):
    r = jnp.arange(tq, dtype=I32)[:, None]
    u = jnp.arange(S, dtype=I32)[None, :]
    d = r + (S - tq) - u
    mult = jnp.zeros((tq, S), F32)
    for window, dil in DILATED_PATTERNS:
        mult = mult + ((d >= 0) & (d % dil == 0) & (d <= window)).astype(F32)
    return jnp.where(mult > 0, jnp.log(jnp.maximum(mult, 1.0)), MASKED)


def _attention(qk3, v3, tq=256, heads_per_step=2):
    B, S, _ = v3.shape
    bias = _attn_bias_table(S, tq)
    H = N_HEADS_ATTN // heads_per_step
    width = heads_per_step * HEAD_DIM
    return pl.pallas_call(
        functools.partial(_attn_kernel, tq=tq),
        grid=(B, H),
        in_specs=[pl.BlockSpec((1, S, width), lambda b, h: (b, 0, h)),
                  pl.BlockSpec((1, S, width), lambda b, h: (b, 0, H + h)),
                  pl.BlockSpec((1, S, width), lambda b, h: (b, 0, h)),
                  pl.BlockSpec((tq, S), lambda b, h: (0, 0))],
        out_specs=pl.BlockSpec((1, S, width), lambda b, h: (b, 0, h)),
        out_shape=jax.ShapeDtypeStruct((B, S, D_ATTN), BF16),
        compiler_params=_params(("parallel", "parallel")),
        name="dilated_attn",
    )(qk3, qk3, v3, bias)


def _gmlp_kernel(a_ref, u_ref, gv_ref, ws_ref, bst_ref, on_ref, o_ref, g_scr):
    tg = a_ref.shape[0]
    row = lax.broadcasted_iota(I32, (GMLP_CHUNK, GMLP_CHUNK), 0)
    col = lax.broadcasted_iota(I32, (GMLP_CHUNK, GMLP_CHUNK), 1)
    causal = row >= col
    for g in range(N_GROUPS_GMLP):
        w = jnp.where(causal, ws_ref[g], 0.0).astype(BF16)
        bias = bst_ref[:, g:g + 1]
        sl = slice(g * HEAD_DIM, (g + 1) * HEAD_DIM)
        for c in range(tg // GMLP_CHUNK):
            rows = slice(c * GMLP_CHUNK, (c + 1) * GMLP_CHUNK)
            mixed = jnp.dot(w, gv_ref[rows, sl], preferred_element_type=F32) + bias
            g_scr[rows, sl] = u_ref[rows, sl].astype(F32) * mixed
    gm = g_scr[...]
    gm = gm * lax.rsqrt(jnp.mean(gm * gm, axis=-1, keepdims=True) + EPS)
    o_ref[:, D_ATTN:] = (gm * on_ref[:, D_ATTN:]).astype(o_ref.dtype)
    a = a_ref[...].astype(F32)
    a = a * lax.rsqrt(jnp.mean(a * a, axis=-1, keepdims=True) + EPS)
    o_ref[:, :D_ATTN] = (a * on_ref[:, :D_ATTN]).astype(o_ref.dtype)


def _gmlp_and_norms(attn2, u, gv, w_s, b_s_t, out_norm, tg=512):
    N = attn2.shape[0]
    D_mix = D_ATTN + D_GMLP
    return pl.pallas_call(
        _gmlp_kernel,
        grid=(N // tg,),
        in_specs=[pl.BlockSpec((tg, D_ATTN), lambda i: (i, 0)),
                  pl.BlockSpec((tg, D_GMLP), lambda i: (i, 0)),
                  pl.BlockSpec((tg, D_GMLP), lambda i: (i, 0)),
                  pl.BlockSpec((N_GROUPS_GMLP, GMLP_CHUNK, GMLP_CHUNK), lambda i: (0, 0, 0)),
                  pl.BlockSpec((GMLP_CHUNK, N_GROUPS_GMLP), lambda i: (0, 0)),
                  pl.BlockSpec((1, D_mix), lambda i: (0, 0))],
        out_specs=pl.BlockSpec((tg, D_mix), lambda i: (i, 0)),
        out_shape=jax.ShapeDtypeStruct((N, D_mix), BF16),
        scratch_shapes=[pltpu.VMEM((tg, D_GMLP), F32)],
        compiler_params=_params(("parallel",)),
        name="gmlp_norms",
    )(attn2, u, gv, w_s, b_s_t, out_norm)


RIDER_MAX_BYTES = 8 * 1024 * 1024


def _rider_rows(total_rows, cols, steps):
    if total_rows % steps:
        return None
    rows = total_rows // steps
    if rows % 16 or rows * cols * 4 > RIDER_MAX_BYTES:
        return None
    return rows


def _outproj_kernel(m_ref, w_ref, x_ref, mb_ref, ml_ref, *rest):
    if len(rest) == 3:
        side_ref, o_ref, side_out = rest
        side_out[...] = side_ref[0].astype(side_out.dtype)
    else:
        o_ref, = rest
    acc = jnp.dot(m_ref[...], w_ref[...], preferred_element_type=F32)
    o_ref[...] = x_ref[...] + _mod_row(mb_ref, ml_ref, 2) * acc


def _out_proj(mix, w_out, x2, mod_base, mod_l, S, side3, layer, bm=1024, bn=512):
    N, D_mix = mix.shape
    D = w_out.shape[1]
    per_b = S // bm
    nj = D // bn
    _, side_rows, side_cols = side3.shape
    rows = _rider_rows(side_rows, side_cols, (N // bm) * nj)
    in_specs = [pl.BlockSpec((bm, D_mix), lambda i, j: (i, 0)),
                pl.BlockSpec((D_mix, bn), lambda i, j: (0, j)),
                pl.BlockSpec((bm, bn), lambda i, j: (i, j)),
                pl.BlockSpec((1, N_MOD, bn), lambda i, j: (i // per_b, 0, j)),
                pl.BlockSpec((N_MOD, bn), lambda i, j: (0, j))]
    out_specs = [pl.BlockSpec((bm, bn), lambda i, j: (i, j))]
    out_shape = [jax.ShapeDtypeStruct((N, D), F32)]
    args = [mix, w_out, x2, mod_base, mod_l]
    if rows is not None:
        in_specs.append(pl.BlockSpec((1, rows, side_cols), lambda i, j: (layer, i * nj + j, 0)))
        out_specs.append(pl.BlockSpec((rows, side_cols), lambda i, j: (i * nj + j, 0)))
        out_shape.append(jax.ShapeDtypeStruct((side_rows, side_cols), BF16))
        args.append(side3)
    res = pl.pallas_call(
        _outproj_kernel,
        grid=(N // bm, nj),
        in_specs=in_specs,
        out_specs=out_specs,
        out_shape=out_shape,
        compiler_params=_params(("parallel", "arbitrary")),
        name="out_proj",
    )(*args)
    return (res[0], res[1]) if rows is not None else (res[0], None)


def _router_kernel(x_ref, mb_ref, ml_ref, wr_ref, br_ref,
                   h_ref, idx_ref, gate_ref, rank_ref, cnt_ref, cnt_scr):
    i = pl.program_id(0)
    tm = x_ref.shape[0]

    @pl.when(i == 0)
    def _():
        cnt_scr[...] = jnp.zeros_like(cnt_scr)

    h = _norm_mod(x_ref[...], mb_ref, ml_ref, 3, 4)
    h_ref[...] = _pack_halves(h)
    logits = jnp.dot(h.astype(BF16), wr_ref[...].astype(BF16),
                     preferred_element_type=F32) + br_ref[...]
    lane = lax.broadcasted_iota(I32, (tm, LANES), 1)
    lane_f = lane.astype(F32)
    vals = jnp.where(lane < N_EXPERTS, logits, -jnp.inf)
    tops, sels = [], []
    idx_out = jnp.zeros((tm, LANES), F32)
    for k in range(TOP_K):
        m = jnp.max(vals, axis=-1, keepdims=True)
        first = jnp.min(jnp.where(vals == m, lane_f, float(LANES)), axis=-1, keepdims=True)
        sel = lane_f == first
        tops.append(m)
        sels.append(sel)
        idx_out = jnp.where(lane == k, first, idx_out)
        vals = jnp.where(sel, -jnp.inf, vals)
    exps = [jnp.exp(t - tops[0]) for t in tops]
    denom = exps[0] + exps[1] + exps[2] + exps[3]
    gate_out = jnp.zeros((tm, LANES), F32)
    for k in range(TOP_K):
        gate_out = jnp.where(lane == k, exps[k] / denom, gate_out)

    chosen = jnp.zeros((tm, LANES), F32)
    for sel in sels:
        chosen = chosen + sel.astype(F32)
    r = lax.broadcasted_iota(I32, (tm, tm), 0)
    c = lax.broadcasted_iota(I32, (tm, tm), 1)
    earlier = jnp.where(r > c, 1.0, 0.0).astype(BF16)
    before = jnp.dot(earlier, chosen.astype(BF16), preferred_element_type=F32) + cnt_scr[...]
    rank_out = jnp.zeros((tm, LANES), F32)
    for k in range(TOP_K):
        rk = jnp.sum(jnp.where(sels[k], before, 0.0), axis=-1, keepdims=True)
        rank_out = jnp.where(lane == k, rk, rank_out)
    cnt_scr[...] = cnt_scr[...] + jnp.sum(chosen, axis=0, keepdims=True)

    idx_ref[...] = idx_out.astype(I32)
    gate_ref[...] = gate_out
    rank_ref[...] = rank_out.astype(I32)
    cnt_ref[...] = cnt_scr[...].astype(I32)


def _ffn_norm_router(x2, mod_base, mod_l, w_router_p, b_router_p, S, tm=256):
    N, D = x2.shape
    per_b = S // tm
    tok_lane = pl.BlockSpec((tm, LANES), lambda i: (i, 0))
    return pl.pallas_call(
        _router_kernel,
        grid=(N // tm,),
        in_specs=[pl.BlockSpec((tm, D), lambda i: (i, 0)),
                  pl.BlockSpec((1, N_MOD, D), lambda i: (i // per_b, 0, 0)),
                  pl.BlockSpec((N_MOD, D), lambda i: (0, 0)),
                  pl.BlockSpec((D, LANES), lambda i: (0, 0)),
                  pl.BlockSpec((1, LANES), lambda i: (0, 0))],
        out_specs=[pl.BlockSpec((tm, D // 2), lambda i: (i, 0)), tok_lane, tok_lane, tok_lane,
                   pl.BlockSpec((1, LANES), lambda i: (0, 0))],
        out_shape=[jax.ShapeDtypeStruct((N, D // 2), jnp.uint32),
                   jax.ShapeDtypeStruct((N, LANES), I32),
                   jax.ShapeDtypeStruct((N, LANES), F32),
                   jax.ShapeDtypeStruct((N, LANES), I32),
                   jax.ShapeDtypeStruct((1, LANES), I32)],
        scratch_shapes=[pltpu.VMEM((1, LANES), F32)],
        compiler_params=_params(("arbitrary",)),
        name="ffn_norm_router",
    )(x2, mod_base, mod_l, w_router_p, b_router_p)


def _split_gate_up_kernel(w_ref, g_ref, u_ref):
    _deinterleave_gate_up(w_ref.at[0, 0], g_ref.at[0], u_ref.at[0])


def _split_gate_up(w_gate_up, layer, tr=1024):
    _, E, D, two_ff = w_gate_up.shape
    d_ff = two_ff // 2
    out = jax.ShapeDtypeStruct((E, D, d_ff), BF16)
    return pl.pallas_call(
        _split_gate_up_kernel,
        grid=(E, D // tr),
        in_specs=[pl.BlockSpec((1, 1, tr, two_ff), lambda e, i: (layer, e, i, 0))],
        out_specs=[pl.BlockSpec((1, tr, d_ff), lambda e, i: (e, i, 0)),
                   pl.BlockSpec((1, tr, d_ff), lambda e, i: (e, i, 0))],
        out_shape=[out, out],
        compiler_params=_params(("parallel", "parallel")),
        name="split_gate_up",
    )(w_gate_up)


def _dispatch_kernel(dest_ref, pad_start_ref, pad_len_ref, nused_ref, h_ref, xs_hbm,
                     zero_buf, sem, pad_sem, blk_sem, *, td, tm):
    i = pl.program_id(0)
    n = pl.num_programs(0)
    n_blocks = xs_hbm.shape[0] // tm

    def body(r, carry):
        base = (i * td + r) * TOP_K
        for k in range(TOP_K):
            pltpu.make_async_copy(h_ref.at[pl.ds(r, 1), :], xs_hbm.at[pl.ds(dest_ref[base + k], 1), :],
                                  sem).start(priority=k % 2)
        return carry
    lax.fori_loop(0, td, body, 0, unroll=2)

    for k in range(TOP_K):
        pltpu.make_async_copy(h_ref, xs_hbm.at[pl.ds(0, td), :], sem).wait()

    @pl.when(i == n - 1)
    def _():
        zero_buf[...] = jnp.zeros_like(zero_buf)

        def per_expert(e, total):
            start = pad_start_ref[e]
            count = pad_len_ref[e]

            def fill(r, carry):
                pltpu.make_async_copy(zero_buf.at[pl.ds(0, 1), :], xs_hbm.at[pl.ds(start + r, 1), :],
                                      pad_sem).start()
                return carry
            lax.fori_loop(0, count, fill, 0)
            return total + count
        total = lax.fori_loop(0, N_EXPERTS, per_expert, 0)

        def block_copy(blk):
            return pltpu.make_async_copy(zero_buf, xs_hbm.at[pl.ds(pl.multiple_of(blk * tm, tm), tm), :],
                                         blk_sem)

        def fill_block(blk, carry):
            block_copy(blk).start()
            return carry
        lax.fori_loop(nused_ref[0], n_blocks, fill_block, 0)

        def retire(r, carry):
            pltpu.make_async_copy(zero_buf.at[pl.ds(0, 1), :], xs_hbm.at[pl.ds(0, 1), :], pad_sem).wait()
            return carry
        lax.fori_loop(0, total, retire, 0)

        def retire_block(blk, carry):
            block_copy(blk).wait()
            return carry
        lax.fori_loop(nused_ref[0], n_blocks, retire_block, 0)


def _dispatch(dest_flat, pad_start, pad_len, nused, h2, n_slots, tm, td=1024):
    N, half = h2.shape
    grid_spec = pltpu.PrefetchScalarGridSpec(
        num_scalar_prefetch=4,
        grid=(N // td,),
        in_specs=[pl.BlockSpec((td, half), lambda i, d, ps, pn, nu: (i, 0))],
        out_specs=pl.BlockSpec(memory_space=pl.ANY),
        scratch_shapes=[pltpu.VMEM((tm, half), jnp.uint32), pltpu.SemaphoreType.DMA(()),
                        pltpu.SemaphoreType.DMA(()), pltpu.SemaphoreType.DMA(())],
    )
    return pl.pallas_call(
        functools.partial(_dispatch_kernel, td=td, tm=tm),
        grid_spec=grid_spec,
        out_shape=jax.ShapeDtypeStruct((n_slots, half), jnp.uint32),
        compiler_params=_params(("arbitrary",)),
        name="moe_dispatch",
    )(dest_flat, pad_start, pad_len, nused, h2)


def _expert_kernel(be_ref, nused_ref, x_ref, wg_ref, wu_ref, wd_ref, bg_ref, bu_ref, bd_ref, y_ref):
    b = pl.program_id(0)

    @pl.when(b < nused_ref[0])
    def _():
        lo, hi = _unpack_halves(x_ref[...])
        x = jnp.concatenate([lo.astype(BF16), hi.astype(BF16)], axis=1)
        gate = jnp.dot(x, wg_ref[0], preferred_element_type=F32) + bg_ref[0]
        up = jnp.dot(x, wu_ref[0], preferred_element_type=F32) + bu_ref[0]
        gate = jnp.minimum(gate, SWIGLU_LIMIT)
        up = jnp.clip(up, -SWIGLU_LIMIT, SWIGLU_LIMIT)
        act = (up + 1.0) * gate * jax.nn.sigmoid(SWIGLU_ALPHA * gate)
        y = jnp.dot(act.astype(BF16), wd_ref[0], preferred_element_type=F32) + bd_ref[0]
        y_ref[...] = _pack_halves(y)

    @pl.when(b >= nused_ref[0])
    def _():
        y_ref[...] = jnp.zeros_like(y_ref)


def _experts(xs, block_e, nused, wg, wu, wd, bg, bu, bd, tm):
    half = xs.shape[1]
    D = 2 * half
    n_blocks = block_e.shape[0]
    d_ff = wg.shape[2]
    e_map = lambda b, be, nu: (be[b], 0, 0)
    x_map = lambda b, be, nu: (jnp.minimum(b, nu[0] - 1), 0)
    grid_spec = pltpu.PrefetchScalarGridSpec(
        num_scalar_prefetch=2,
        grid=(n_blocks,),
        in_specs=[pl.BlockSpec((tm, half), x_map),
                  pl.BlockSpec((1, D, d_ff), e_map),
                  pl.BlockSpec((1, D, d_ff), e_map),
                  pl.BlockSpec((1, d_ff, D), e_map),
                  pl.BlockSpec((1, 1, d_ff), e_map),
                  pl.BlockSpec((1, 1, d_ff), e_map),
                  pl.BlockSpec((1, 1, D), e_map)],
        out_specs=pl.BlockSpec((tm, half), lambda b, be, nu: (b, 0)),
    )
    return pl.pallas_call(
        _expert_kernel,
        grid_spec=grid_spec,
        out_shape=jax.ShapeDtypeStruct((n_blocks * tm, half), jnp.uint32),
        compiler_params=_params(("arbitrary",)),
        name="experts",
    )(block_e, nused, xs, wg, wu, wd, bg, bu, bd)


SUBLANES = 8


def _combine_kernel(dest_ref, y_hbm, x_ref, gate_ref, mb_ref, ml_ref, o_ref, buf0, buf1, sem, *, tc):
    i = pl.program_id(0)
    n = pl.num_programs(0)
    bufs = (buf0, buf1)
    half = buf0.shape[2]

    def row_copy(d, slot, k, r):
        return pltpu.make_async_copy(y_hbm.at[pl.ds(d, 1), :],
                                     bufs[slot].at[k, pl.ds(r, 1), :], sem.at[slot])

    def issue_rows(step, slot, r0, count):
        for rr in range(count):
            base = (step * tc + r0 + rr) * TOP_K
            for k in range(TOP_K):
                row_copy(dest_ref[base + k], slot, k, r0 + rr).start(priority=k % 2)

    def wait_step(slot):
        for k in range(TOP_K):
            pltpu.make_async_copy(y_hbm.at[pl.ds(0, tc), :], bufs[slot].at[k], sem.at[slot]).wait()

    @pl.when(i == 0)
    def _():
        def body(c, carry):
            issue_rows(0, 0, c * SUBLANES, SUBLANES)
            return carry
        lax.fori_loop(0, tc // SUBLANES, body, 0)

    nxt = jnp.minimum(i + 1, n - 1)
    gate_f = _mod_row(mb_ref, ml_ref, 5)

    def step(slot):
        wait_step(slot)

        def body(c, carry):
            r0 = pl.multiple_of(c * SUBLANES, SUBLANES)
            issue_rows(nxt, 1 - slot, r0, SUBLANES)
            rows = pl.ds(r0, SUBLANES)
            acc_lo = jnp.zeros((SUBLANES, half), F32)
            acc_hi = jnp.zeros((SUBLANES, half), F32)
            for k in range(TOP_K):
                lo, hi = _unpack_halves(bufs[slot][k, rows, :])
                g = gate_ref[rows, k:k + 1]
                acc_lo = acc_lo + g * lo
                acc_hi = acc_hi + g * hi
            o_ref[rows, :half] = x_ref[rows, :half] + gate_f[:, :half] * acc_lo
            o_ref[rows, half:] = x_ref[rows, half:] + gate_f[:, half:] * acc_hi
            return carry
        lax.fori_loop(0, tc // SUBLANES, body, 0)

        @pl.when(i == n - 1)
        def _():
            wait_step(1 - slot)

    for slot in range(2):
        @pl.when(i % 2 == slot)
        def _(slot=slot):
            step(slot)


def _combine(dest_flat, y, x2, gates, mod_base, mod_l, S, tc=256):
    N, D = x2.shape
    per_b = S // tc
    grid_spec = pltpu.PrefetchScalarGridSpec(
        num_scalar_prefetch=1,
        grid=(N // tc,),
        in_specs=[pl.BlockSpec(memory_space=pl.ANY),
                  pl.BlockSpec((tc, D), lambda i, d: (i, 0)),
                  pl.BlockSpec((tc, LANES), lambda i, d: (i, 0)),
                  pl.BlockSpec((1, N_MOD, D), lambda i, d: (i // per_b, 0, 0)),
                  pl.BlockSpec((N_MOD, D), lambda i, d: (0, 0))],
        out_specs=pl.BlockSpec((tc, D), lambda i, d: (i, 0)),
        scratch_shapes=[pltpu.VMEM((TOP_K, tc, D // 2), jnp.uint32),
                        pltpu.VMEM((TOP_K, tc, D // 2), jnp.uint32), pltpu.SemaphoreType.DMA((2,))],
    )
    return pl.pallas_call(
        functools.partial(_combine_kernel, tc=tc),
        grid_spec=grid_spec,
        out_shape=jax.ShapeDtypeStruct((N, D), F32),
        compiler_params=_params(("arbitrary",)),
        name="moe_combine",
    )(dest_flat, y, x2, gates, mod_base, mod_l)


EXPERT_ROWS = 256


def _rope_tables(positions):
    half = ROT_DIM // 2
    inv_freq = ROPE_THETA ** (-jnp.arange(half, dtype=F32) * 2.0 / ROT_DIM)
    ang = positions.astype(F32).reshape(-1, 1) * inv_freq
    cos, sin = jnp.cos(ang), jnp.sin(ang)
    n = ang.shape[0]
    ones = jnp.ones((n, HEAD_DIM - ROT_DIM), F32)
    zeros = jnp.zeros((n, HEAD_DIM - half), F32)
    cos_t = jnp.concatenate([cos, cos, ones], axis=1)
    s1_t = jnp.concatenate([-sin, zeros], axis=1)
    s2_t = jnp.concatenate([jnp.zeros((n, half), F32), sin,
                            jnp.zeros((n, HEAD_DIM - ROT_DIM), F32)], axis=1)
    return cos_t, s1_t, s2_t


def _routing_tables(top_idx, rank, counts, tm, n_blocks):
    N = top_idx.shape[0]
    nblk = (counts + tm - 1) // tm
    blk_end = jnp.cumsum(nblk)
    blk_start = blk_end - nblk
    dest = blk_start[top_idx] * tm + rank
    owner = jnp.sum((blk_end[None, :] <= jnp.arange(n_blocks, dtype=I32)[:, None]).astype(I32), axis=1)
    block_e = jnp.minimum(owner, N_EXPERTS - 1).astype(I32)
    nused = blk_end[-1:].astype(I32)
    pad_start = (blk_start * tm + counts).astype(I32)
    pad_len = (nblk * tm - counts).astype(I32)
    return dest.reshape(-1).astype(I32), block_e, nused, pad_start, pad_len


def kernel(x, c, positions, w_mod, b_mod, mod_layer, w_in, q_norm, k_norm, w_s, b_s, v_ln_g, v_ln_b,
           out_norm, w_out, w_router, b_router, w_gate_up, b_gate_up, w_down, b_down):
    B, S, D = x.shape
    N = B * S
    depth = w_in.shape[0]
    d_ff = w_down.shape[2]
    tm_e = EXPERT_ROWS
    n_blocks = N * TOP_K // tm_e + N_EXPERTS

    mod_base = _mod_base(c, w_mod, b_mod).reshape(B, N_MOD, D)
    cos_t, s1_t, s2_t = _rope_tables(positions)
    x2 = x.reshape(N, D)

    for l in range(depth):
        mod_l = mod_layer[l]
        h = _mixer_norm(x2, mod_base, mod_l, S)
        qk_gain = jnp.stack([q_norm[l] * HEAD_DIM ** -0.5, k_norm[l]]).reshape(2, 1, HEAD_DIM)
        qk, v, u, gv, wg, wu = _in_proj(h, _to_bf16(w_in, l, 256), qk_gain, cos_t, s1_t, s2_t,
                                        v_ln_g[l].reshape(1, D_GMLP), v_ln_b[l].reshape(1, D_GMLP),
                                        w_gate_up.reshape(depth, N_EXPERTS * D, 2 * d_ff), l)
        attn = _attention(qk.reshape(B, S, -1), v.reshape(B, S, -1))
        mix = _gmlp_and_norms(attn.reshape(N, D_ATTN), u, gv, w_s[l], jnp.transpose(b_s[l]),
                              out_norm[l].reshape(1, -1))
        w_down3 = w_down.reshape(depth, N_EXPERTS * d_ff, D)
        x2, wd = _out_proj(mix, _to_bf16(w_out, l, 512), x2, mod_base, mod_l, S, w_down3, l)

        w_router_p = jnp.pad(w_router[l], ((0, 0), (0, LANES - N_EXPERTS)))
        b_router_p = jnp.pad(b_router[l], (0, LANES - N_EXPERTS)).reshape(1, LANES)
        h2, top_idx, gates, rank, counts = _ffn_norm_router(x2, mod_base, mod_l, w_router_p,
                                                            b_router_p, S)
        dest, block_e, nused, pad_start, pad_len = _routing_tables(
            top_idx[:, :TOP_K], rank[:, :TOP_K], counts[0, :N_EXPERTS], tm_e, n_blocks)
        xs = _dispatch(dest, pad_start, pad_len, nused, h2, n_blocks * tm_e, tm_e)
        if wg is None:
            wg, wu = _split_gate_up(w_gate_up, l)
        if wd is None:
            wd = _to_bf16(w_down3, l, 512)
        wg = wg.reshape(N_EXPERTS, D, d_ff)
        wu = wu.reshape(N_EXPERTS, D, d_ff)
        wd = wd.reshape(N_EXPERTS, d_ff, D)
        bg = b_gate_up[l][:, 0::2].reshape(N_EXPERTS, 1, d_ff)
        bu = b_gate_up[l][:, 1::2].reshape(N_EXPERTS, 1, d_ff)
        y = _experts(xs, block_e, nused, wg, wu, wd, bg, bu,
                     b_down[l].reshape(N_EXPERTS, 1, D), tm_e)
        x2 = _combine(dest, y, x2, gates, mod_base, mod_l, S)
    return x2.reshape(B, S, D)
```

```python
import functools
import math

import jax
import jax.numpy as jnp
from jax import lax
from jax.experimental import pallas as pl
from jax.experimental.pallas import tpu as pltpu

F32 = jnp.float32
BF16 = jnp.bfloat16
I32 = jnp.int32

HEAD_DIM = 128
N_HEADS_ATTN = 16
N_GROUPS_GMLP = 16
D_ATTN = N_HEADS_ATTN * HEAD_DIM
D_GMLP = N_GROUPS_GMLP * HEAD_DIM
ROT_DIM = HEAD_DIM // 4
ROPE_THETA = 500000.0
DILATED_PATTERNS = ((128, 1), (512, 4), (2048, 16))
GMLP_CHUNK = 128
N_EXPERTS = 32
TOP_K = 4
SWIGLU_LIMIT = 7.0
SWIGLU_ALPHA = 1.702
N_MOD = 6
EPS = 1e-6
LANES = 128
MASKED = -1e30

VMEM_LIMIT = 56 * 1024 * 1024


def _params(sem, vmem=VMEM_LIMIT):
    return pltpu.CompilerParams(dimension_semantics=sem, vmem_limit_bytes=vmem)


def _mod_kernel(c_ref, w_ref, b_ref, o_ref):
    c = c_ref[...]
    s = c * jax.nn.sigmoid(c)
    o_ref[...] = jnp.dot(s.astype(BF16), w_ref[...].astype(BF16),
                         preferred_element_type=F32) + b_ref[...]


def _mod_base(c, w_mod, b_mod, tn=512):
    B, D = c.shape
    n_out = w_mod.shape[1]
    return pl.pallas_call(
        _mod_kernel,
        grid=(n_out // tn,),
        in_specs=[pl.BlockSpec((B, D), lambda j: (0, 0)),
                  pl.BlockSpec((D, tn), lambda j: (0, j)),
                  pl.BlockSpec((1, tn), lambda j: (0, j))],
        out_specs=pl.BlockSpec((B, tn), lambda j: (0, j)),
        out_shape=jax.ShapeDtypeStruct((B, n_out), F32),
        compiler_params=_params(("parallel",)),
        name="mod_base",
    )(c, w_mod, b_mod.reshape(1, n_out))


def _pack_halves(x):
    c = x.shape[1] // 2
    bits = pltpu.bitcast(x.astype(BF16).astype(F32), jnp.uint32)
    return (bits[:, c:] & jnp.uint32(0xFFFF0000)) | (bits[:, :c] >> 16)


def _unpack_halves(p):
    lo = pltpu.bitcast(p << 16, F32)
    hi = pltpu.bitcast(p & jnp.uint32(0xFFFF0000), F32)
    return lo, hi


def _mod_row(mb_ref, ml_ref, idx):
    return mb_ref[0, idx:idx + 1, :] + ml_ref[idx:idx + 1, :]


def _norm_mod(x, mb_ref, ml_ref, shift_idx, scale_idx):
    y = x * lax.rsqrt(jnp.mean(x * x, axis=-1, keepdims=True) + EPS)
    return y * (1.0 + _mod_row(mb_ref, ml_ref, scale_idx)) + _mod_row(mb_ref, ml_ref, shift_idx)


def _norm_kernel(x_ref, mb_ref, ml_ref, h_ref):
    h_ref[...] = _norm_mod(x_ref[...], mb_ref, ml_ref, 0, 1).astype(h_ref.dtype)


def _mixer_norm(x2, mod_base, mod_l, S, tm=256):
    N, D = x2.shape
    per_b = S // tm
    return pl.pallas_call(
        _norm_kernel,
        grid=(N // tm,),
        in_specs=[pl.BlockSpec((tm, D), lambda i: (i, 0)),
                  pl.BlockSpec((1, N_MOD, D), lambda i: (i // per_b, 0, 0)),
                  pl.BlockSpec((N_MOD, D), lambda i: (0, 0))],
        out_specs=pl.BlockSpec((tm, D), lambda i: (i, 0)),
        out_shape=jax.ShapeDtypeStruct((N, D), BF16),
        compiler_params=_params(("parallel",)),
        name="mixer_norm",
    )(x2, mod_base, mod_l)


def _gelu(x):
    return 0.5 * x * (1.0 + lax.erf(x * (1.0 / math.sqrt(2.0))))


def _proj_kernel(*refs, mode, sub):
    if mode == "gelu_ln":
        h_ref, w_ref, lg_ref, lb_ref, o_ref = refs
    else:
        h_ref, w_ref, o_ref = refs
    bn = o_ref.shape[1]
    for s in range(bn // sub):
        acc = jnp.dot(h_ref[...], w_ref[:, s * sub:(s + 1) * sub], preferred_element_type=F32)
        for hh in range(sub // HEAD_DIM):
            t = acc[:, hh * HEAD_DIM:(hh + 1) * HEAD_DIM]
            sl = slice(s * sub + hh * HEAD_DIM, s * sub + (hh + 1) * HEAD_DIM)
            if mode == "gelu":
                r = _gelu(t)
            elif mode == "gelu_ln":
                g = _gelu(t)
                gc = g - jnp.mean(g, axis=-1, keepdims=True)
                r = gc * lax.rsqrt(jnp.mean(gc * gc, axis=-1, keepdims=True) + EPS)
                r = r * lg_ref[:, sl] + lb_ref[:, sl]
            else:
                r = t
            o_ref[:, sl] = r.astype(o_ref.dtype)


def _proj(h, w, col0, width, mode, extra=(), extra_specs=(), bm=1024, bn=1024, sub=256):
    N, D = h.shape
    j0 = col0 // bn
    return pl.pallas_call(
        functools.partial(_proj_kernel, mode=mode, sub=sub),
        grid=(N // bm, width // bn),
        in_specs=[pl.BlockSpec((bm, D), lambda i, j: (i, 0)),
                  pl.BlockSpec((D, bn), lambda i, j: (0, j0 + j))] + list(extra_specs),
        out_specs=pl.BlockSpec((bm, bn), lambda i, j: (i, j)),
        out_shape=jax.ShapeDtypeStruct((N, width), BF16),
        compiler_params=_params(("parallel", "arbitrary")),
        name="in_proj_" + mode,
    )(h, w, *extra)


MXU_DIM = 256


def _deinterleave_gate_up(w2, g2, u2):
    r = lax.broadcasted_iota(I32, (MXU_DIM, MXU_DIM), 0)
    c = lax.broadcasted_iota(I32, (MXU_DIM, MXU_DIM), 1)
    src = jnp.where(c < LANES, 2 * c, 2 * (c - LANES) + 1)
    perm = jnp.where(r == src, 1.0, 0.0).astype(BF16)
    for ch in range(w2.shape[1] // MXU_DIM):
        w = w2[:, ch * MXU_DIM:(ch + 1) * MXU_DIM].astype(BF16)
        res = jnp.dot(w, perm, preferred_element_type=F32)
        g2[:, ch * LANES:(ch + 1) * LANES] = res[:, :LANES].astype(BF16)
        u2[:, ch * LANES:(ch + 1) * LANES] = res[:, LANES:].astype(BF16)


def _qk_proj_kernel(h_ref, w_ref, gain_ref, cos_ref, s1_ref, s2_ref, *rest):
    if len(rest) == 6:
        side_ref, o_ref, g_ref, u_ref, acc0, acc1 = rest
    else:
        side_ref = None
        o_ref, acc0, acc1 = rest
    t = pl.program_id(0)
    bn = o_ref.shape[1]

    @pl.when(t == 0)
    def _():
        acc1[...] = jnp.zeros_like(acc1)

    def step(acc_mm, acc_ep):
        acc_mm[...] = jnp.dot(h_ref[...], w_ref[...], preferred_element_type=F32)
        if side_ref is not None:
            _deinterleave_gate_up(side_ref.at[0], g_ref, u_ref)
        for hh in range(bn // HEAD_DIM):
            sl = slice(hh * HEAD_DIM, (hh + 1) * HEAD_DIM)
            a = acc_ep[:, sl]
            y = a * lax.rsqrt(jnp.mean(a * a, axis=-1, keepdims=True) + EPS) * gain_ref[0]
            r = (y * cos_ref[...] + pltpu.roll(y, HEAD_DIM - ROT_DIM // 2, 1) * s1_ref[...]
                 + pltpu.roll(y, ROT_DIM // 2, 1) * s2_ref[...])
            o_ref[:, sl] = r.astype(o_ref.dtype)

    @pl.when(t % 2 == 0)
    def _():
        step(acc0, acc1)

    @pl.when(t % 2 == 1)
    def _():
        step(acc1, acc0)


def _qk_proj(h, w, qk_gain, cos_t, s1_t, s2_t, side3, layer, bm=1024, bn=512):
    N, D = h.shape
    width = 2 * D_ATTN
    nj = width // bn
    n_tiles = (N // bm) * nj
    tiles_per_part = D_ATTN // bn
    mm = lambda t: jnp.minimum(t, n_tiles - 1)
    ep = lambda t: jnp.maximum(t - 1, 0)
    tok_tab = pl.BlockSpec((bm, HEAD_DIM), lambda t: (ep(t) // nj, 0))
    _, side_rows, side_cols = side3.shape
    rows = _rider_rows(side_rows, side_cols, n_tiles)
    in_specs = [pl.BlockSpec((bm, D), lambda t: (mm(t) // nj, 0)),
                pl.BlockSpec((D, bn), lambda t: (0, mm(t) % nj)),
                pl.BlockSpec((1, 1, HEAD_DIM), lambda t: ((ep(t) % nj) // tiles_per_part, 0, 0)),
                tok_tab, tok_tab, tok_tab]
    out_specs = [pl.BlockSpec((bm, bn), lambda t: (ep(t) // nj, ep(t) % nj))]
    out_shape = [jax.ShapeDtypeStruct((N, width), BF16)]
    args = [h, w, qk_gain, cos_t, s1_t, s2_t]
    if rows is not None:
        half_spec = pl.BlockSpec((rows, side_cols // 2), lambda t: (mm(t), 0))
        in_specs.append(pl.BlockSpec((1, rows, side_cols), lambda t: (layer, mm(t), 0)))
        out_specs += [half_spec, half_spec]
        out_shape += [jax.ShapeDtypeStruct((side_rows, side_cols // 2), BF16)] * 2
        args.append(side3)
    res = pl.pallas_call(
        _qk_proj_kernel,
        grid=(n_tiles + 1,),
        in_specs=in_specs,
        out_specs=out_specs,
        out_shape=out_shape,
        scratch_shapes=[pltpu.VMEM((bm, bn), F32), pltpu.VMEM((bm, bn), F32)],
        compiler_params=_params(("arbitrary",)),
        name="in_proj_qk",
    )(*args)
    return (res[0], res[1], res[2]) if rows is not None else (res[0], None, None)


def _cast_kernel(x_ref, o_ref):
    o_ref[...] = x_ref[0].astype(o_ref.dtype)


def _to_bf16(w3, layer, tr):
    _, R, C = w3.shape
    return pl.pallas_call(
        _cast_kernel,
        grid=(R // tr,),
        in_specs=[pl.BlockSpec((1, tr, C), lambda i: (layer, i, 0))],
        out_specs=pl.BlockSpec((tr, C), lambda i: (i, 0)),
        out_shape=jax.ShapeDtypeStruct((R, C), BF16),
        compiler_params=_params(("parallel",)),
        name="to_bf16",
    )(w3)


def _in_proj(h, w_in, qk_gain, cos_t, s1_t, s2_t, ln_g, ln_b, gate_up3, layer, bm=1024, bn=1024):
    qk, wg, wu = _qk_proj(h, w_in, qk_gain, cos_t, s1_t, s2_t, gate_up3, layer)
    v = _proj(h, w_in, 2 * D_ATTN, D_ATTN, "none", bm=bm, bn=bn)
    u = _proj(h, w_in, 3 * D_ATTN, D_GMLP, "gelu", bm=bm, bn=bn)
    row = pl.BlockSpec((1, bn), lambda i, j: (0, j))
    gv = _proj(h, w_in, 3 * D_ATTN + D_GMLP, D_GMLP, "gelu_ln", (ln_g, ln_b), (row, row), bm, bn)
    return qk, v, u, gv, wg, wu


def _attn_kernel(q_ref, k_ref, v_ref, bias_ref, *rest, tq):
    n_cast = (len(rest) - 1) // 2
    o_ref = rest[n_cast]
    for src, dst in zip(rest[:n_cast], rest[n_cast + 1:]):
        dst[...] = src[0].astype(dst.dtype)
    S = q_ref.shape[1]
    n_q = S // tq
    for i in range(n_q):
        kend = (i + 1) * tq
        off = (n_q - 1 - i) * tq
        for hh in range(q_ref.shape[2] // HEAD_DIM):
            cols = slice(hh * HEAD_DIM, (hh + 1) * HEAD_DIM)
            q = q_ref[0, i * tq:(i + 1) * tq, cols]
            s = lax.dot_general(q, k_ref[0, :kend, cols], (((1,), (1,)), ((), ())),
                                preferred_element_type=F32)
            s = s + bias_ref[:, off:off + kend]
            m = jnp.max(s, axis=-1, keepdims=True)
            p = jnp.exp(s - m)
            l = jnp.sum(p, axis=-1, keepdims=True)
            o = jnp.dot(p.astype(BF16), v_ref[0, :kend, cols], preferred_element_type=F32)
            o_ref[0, i * tq:(i + 1) * tq, cols] = (o / l).astype(o_ref.dtype)


def _attn_bias_table(S, tq):
    r = jnp.arange(tq, dtype=I32)[:, None]
    u = jnp.arange(S, dtype=I32)[None, :]
    d = r + (S - tq) - u
    mult = jnp.zeros((tq, S), F32)
    for window, dil in DILATED_PATTERNS:
        mult = mult + ((d >= 0) & (d % dil == 0) & (d <= window)).astype(F32)
    return jnp.where(mult > 0, jnp.log(jnp.maximum(mult, 1.0)), MASKED)


def _attention(qk3, v3, casts=(), tq=256, heads_per_step=2):
    B, S, _ = v3.shape
    bias = _attn_bias_table(S, tq)
    H = N_HEADS_ATTN // heads_per_step
    width = heads_per_step * HEAD_DIM
    head_blk = pl.BlockSpec((1, S, width), lambda b, h: (b, 0, h))
    in_specs = [head_blk, pl.BlockSpec((1, S, width), lambda b, h: (b, 0, H + h)), head_blk,
                pl.BlockSpec((tq, S), lambda b, h: (0, 0))]
    out_specs = [head_blk]
    out_shape = [jax.ShapeDtypeStruct((B, S, D_ATTN), BF16)]
    args = [qk3, qk3, v3, bias]
    riding = []
    for arr3, layer in casts:
        _, total_rows, cols = arr3.shape
        rows = _rider_rows(total_rows, cols, B * H)
        riding.append(rows is not None)
        if rows is not None:
            in_specs.append(pl.BlockSpec((1, rows, cols), lambda b, h, layer=layer: (layer, b * H + h, 0)))
            out_specs.append(pl.BlockSpec((rows, cols), lambda b, h: (b * H + h, 0)))
            out_shape.append(jax.ShapeDtypeStruct((total_rows, cols), BF16))
            args.append(arr3)
    res = pl.pallas_call(
        functools.partial(_attn_kernel, tq=tq),
        grid=(B, H),
        in_specs=in_specs,
        out_specs=out_specs,
        out_shape=out_shape,
        compiler_params=_params(("parallel", "parallel")),
        name="dilated_attn",
    )(*args)
    outs = iter(res[1:])
    return res[0], [next(outs) if r else None for r in riding]


def _gmlp_kernel(a_ref, u_ref, gv_ref, ws_ref, bst_ref, on_ref, o_ref, g_scr):
    tg = a_ref.shape[0]
    row = lax.broadcasted_iota(I32, (GMLP_CHUNK, GMLP_CHUNK), 0)
    col = lax.broadcasted_iota(I32, (GMLP_CHUNK, GMLP_CHUNK), 1)
    causal = row >= col
    for g in range(N_GROUPS_GMLP):
        w = jnp.where(causal, ws_ref[g], 0.0).astype(BF16)
        bias = bst_ref[:, g:g + 1]
        sl = slice(g * HEAD_DIM, (g + 1) * HEAD_DIM)
        for c in range(tg // GMLP_CHUNK):
            rows = slice(c * GMLP_CHUNK, (c + 1) * GMLP_CHUNK)
            mixed = jnp.dot(w, gv_ref[rows, sl], preferred_element_type=F32) + bias
            g_scr[rows, sl] = u_ref[rows, sl].astype(F32) * mixed
    gm = g_scr[...]
    gm = gm * lax.rsqrt(jnp.mean(gm * gm, axis=-1, keepdims=True) + EPS)
    o_ref[:, D_ATTN:] = (gm * on_ref[:, D_ATTN:]).astype(o_ref.dtype)
    a = a_ref[...].astype(F32)
    a = a * lax.rsqrt(jnp.mean(a * a, axis=-1, keepdims=True) + EPS)
    o_ref[:, :D_ATTN] = (a * on_ref[:, :D_ATTN]).astype(o_ref.dtype)


def _gmlp_and_norms(attn2, u, gv, w_s, b_s_t, out_norm, tg=512):
    N = attn2.shape[0]
    D_mix = D_ATTN + D_GMLP
    return pl.pallas_call(
        _gmlp_kernel,
        grid=(N // tg,),
        in_specs=[pl.BlockSpec((tg, D_ATTN), lambda i: (i, 0)),
                  pl.BlockSpec((tg, D_GMLP), lambda i: (i, 0)),
                  pl.BlockSpec((tg, D_GMLP), lambda i: (i, 0)),
                  pl.BlockSpec((N_GROUPS_GMLP, GMLP_CHUNK, GMLP_CHUNK), lambda i: (0, 0, 0)),
                  pl.BlockSpec((GMLP_CHUNK, N_GROUPS_GMLP), lambda i: (0, 0)),
                  pl.BlockSpec((1, D_mix), lambda i: (0, 0))],
        out_specs=pl.BlockSpec((tg, D_mix), lambda i: (i, 0)),
        out_shape=jax.ShapeDtypeStruct((N, D_mix), BF16),
        scratch_shapes=[pltpu.VMEM((tg, D_GMLP), F32)],
        compiler_params=_params(("parallel",)),
        name="gmlp_norms",
    )(attn2, u, gv, w_s, b_s_t, out_norm)


RIDER_MAX_BYTES = 8 * 1024 * 1024


def _rider_rows(total_rows, cols, steps):
    if total_rows % steps:
        return None
    rows = total_rows // steps
    if rows % 16 or rows * cols * 4 > RIDER_MAX_BYTES:
        return None
    return rows


def _outproj_kernel(m_ref, w_ref, x_ref, mb_ref, ml_ref, *rest):
    if len(rest) == 3:
        side_ref, o_ref, side_out = rest
        side_out[...] = side_ref[0].astype(side_out.dtype)
    else:
        o_ref, = rest
    acc = jnp.dot(m_ref[...], w_ref[...], preferred_element_type=F32)
    o_ref[...] = x_ref[...] + _mod_row(mb_ref, ml_ref, 2) * acc


def _out_proj(mix, w_out, x2, mod_base, mod_l, S, side3, layer, bm=1024, bn=512):
    N, D_mix = mix.shape
    D = w_out.shape[1]
    per_b = S // bm
    nj = D // bn
    _, side_rows, side_cols = side3.shape
    rows = _rider_rows(side_rows, side_cols, (N // bm) * nj)
    in_specs = [pl.BlockSpec((bm, D_mix), lambda i, j: (i, 0)),
                pl.BlockSpec((D_mix, bn), lambda i, j: (0, j)),
                pl.BlockSpec((bm, bn), lambda i, j: (i, j)),
                pl.BlockSpec((1, N_MOD, bn), lambda i, j: (i // per_b, 0, j)),
                pl.BlockSpec((N_MOD, bn), lambda i, j: (0, j))]
    out_specs = [pl.BlockSpec((bm, bn), lambda i, j: (i, j))]
    out_shape = [jax.ShapeDtypeStruct((N, D), F32)]
    args = [mix, w_out, x2, mod_base, mod_l]
    if rows is not None:
        in_specs.append(pl.BlockSpec((1, rows, side_cols), lambda i, j: (layer, i * nj + j, 0)))
        out_specs.append(pl.BlockSpec((rows, side_cols), lambda i, j: (i * nj + j, 0)))
        out_shape.append(jax.ShapeDtypeStruct((side_rows, side_cols), BF16))
        args.append(side3)
    res = pl.pallas_call(
        _outproj_kernel,
        grid=(N // bm, nj),
        in_specs=in_specs,
        out_specs=out_specs,
        out_shape=out_shape,
        compiler_params=_params(("parallel", "arbitrary")),
        name="out_proj",
    )(*args)
    return (res[0], res[1]) if rows is not None else (res[0], None)


def _router_kernel(x_ref, mb_ref, ml_ref, wr_ref, br_ref,
                   h_ref, idx_ref, gate_ref, rank_ref, cnt_ref, cnt_scr):
    i = pl.program_id(0)
    tm = x_ref.shape[0]

    @pl.when(i == 0)
    def _():
        cnt_scr[...] = jnp.zeros_like(cnt_scr)

    h = _norm_mod(x_ref[...], mb_ref, ml_ref, 3, 4)
    h_ref[...] = _pack_halves(h)
    logits = jnp.dot(h.astype(BF16), wr_ref[...].astype(BF16),
                     preferred_element_type=F32) + br_ref[...]
    lane = lax.broadcasted_iota(I32, (tm, LANES), 1)
    lane_f = lane.astype(F32)
    vals = jnp.where(lane < N_EXPERTS, logits, -jnp.inf)
    tops, sels = [], []
    idx_out = jnp.zeros((tm, LANES), F32)
    for k in range(TOP_K):
        m = jnp.max(vals, axis=-1, keepdims=True)
        first = jnp.min(jnp.where(vals == m, lane_f, float(LANES)), axis=-1, keepdims=True)
        sel = lane_f == first
        tops.append(m)
        sels.append(sel)
        idx_out = jnp.where(lane == k, first, idx_out)
        vals = jnp.where(sel, -jnp.inf, vals)
    exps = [jnp.exp(t - tops[0]) for t in tops]
    denom = exps[0] + exps[1] + exps[2] + exps[3]
    gate_out = jnp.zeros((tm, LANES), F32)
    for k in range(TOP_K):
        gate_out = jnp.where(lane == k, exps[k] / denom, gate_out)

    chosen = jnp.zeros((tm, LANES), F32)
    for sel in sels:
        chosen = chosen + sel.astype(F32)
    r = lax.broadcasted_iota(I32, (tm, tm), 0)
    c = lax.broadcasted_iota(I32, (tm, tm), 1)
    earlier = jnp.where(r > c, 1.0, 0.0).astype(BF16)
    before = jnp.dot(earlier, chosen.astype(BF16), preferred_element_type=F32) + cnt_scr[...]
    rank_out = jnp.zeros((tm, LANES), F32)
    for k in range(TOP_K):
        rk = jnp.sum(jnp.where(sels[k], before, 0.0), axis=-1, keepdims=True)
        rank_out = jnp.where(lane == k, rk, rank_out)
    cnt_scr[...] = cnt_scr[...] + jnp.sum(chosen, axis=0, keepdims=True)

    idx_ref[...] = idx_out.astype(I32)
    gate_ref[...] = gate_out
    rank_ref[...] = rank_out.astype(I32)
    cnt_ref[...] = cnt_scr[...].astype(I32)


def _ffn_norm_router(x2, mod_base, mod_l, w_router_p, b_router_p, S, tm=256):
    N, D = x2.shape
    per_b = S // tm
    tok_lane = pl.BlockSpec((tm, LANES), lambda i: (i, 0))
    return pl.pallas_call(
        _router_kernel,
        grid=(N // tm,),
        in_specs=[pl.BlockSpec((tm, D), lambda i: (i, 0)),
                  pl.BlockSpec((1, N_MOD, D), lambda i: (i // per_b, 0, 0)),
                  pl.BlockSpec((N_MOD, D), lambda i: (0, 0)),
                  pl.BlockSpec((D, LANES), lambda i: (0, 0)),
                  pl.BlockSpec((1, LANES), lambda i: (0, 0))],
        out_specs=[pl.BlockSpec((tm, D // 2), lambda i: (i, 0)), tok_lane, tok_lane, tok_lane,
                   pl.BlockSpec((1, LANES), lambda i: (0, 0))],
        out_shape=[jax.ShapeDtypeStruct((N, D // 2), jnp.uint32),
                   jax.ShapeDtypeStruct((N, LANES), I32),
                   jax.ShapeDtypeStruct((N, LANES), F32),
                   jax.ShapeDtypeStruct((N, LANES), I32),
                   jax.ShapeDtypeStruct((1, LANES), I32)],
        scratch_shapes=[pltpu.VMEM((1, LANES), F32)],
        compiler_params=_params(("arbitrary",)),
        name="ffn_norm_router",
    )(x2, mod_base, mod_l, w_router_p, b_router_p)


def _split_gate_up_kernel(w_ref, g_ref, u_ref):
    _deinterleave_gate_up(w_ref.at[0, 0], g_ref.at[0], u_ref.at[0])


def _split_gate_up(w_gate_up, layer, tr=1024):
    _, E, D, two_ff = w_gate_up.shape
    d_ff = two_ff // 2
    out = jax.ShapeDtypeStruct((E, D, d_ff), BF16)
    return pl.pallas_call(
        _split_gate_up_kernel,
        grid=(E, D // tr),
        in_specs=[pl.BlockSpec((1, 1, tr, two_ff), lambda e, i: (layer, e, i, 0))],
        out_specs=[pl.BlockSpec((1, tr, d_ff), lambda e, i: (e, i, 0)),
                   pl.BlockSpec((1, tr, d_ff), lambda e, i: (e, i, 0))],
        out_shape=[out, out],
        compiler_params=_params(("parallel", "parallel")),
        name="split_gate_up",
    )(w_gate_up)


def _dispatch_kernel(dest_ref, pad_start_ref, pad_len_ref, nused_ref, h_ref, xs_hbm,
                     zero_buf, sem, pad_sem, blk_sem, *, td, tm):
    i = pl.program_id(0)
    n = pl.num_programs(0)
    n_blocks = xs_hbm.shape[0] // tm

    def body(r, carry):
        base = (i * td + r) * TOP_K
        for k in range(TOP_K):
            pltpu.make_async_copy(h_ref.at[pl.ds(r, 1), :], xs_hbm.at[pl.ds(dest_ref[base + k], 1), :],
                                  sem).start(priority=k % 2)
        return carry
    lax.fori_loop(0, td, body, 0, unroll=2)

    for k in range(TOP_K):
        pltpu.make_async_copy(h_ref, xs_hbm.at[pl.ds(0, td), :], sem).wait()

    @pl.when(i == n - 1)
    def _():
        zero_buf[...] = jnp.zeros_like(zero_buf)

        def per_expert(e, total):
            start = pad_start_ref[e]
            count = pad_len_ref[e]

            def fill(r, carry):
                pltpu.make_async_copy(zero_buf.at[pl.ds(0, 1), :], xs_hbm.at[pl.ds(start + r, 1), :],
                                      pad_sem).start()
                return carry
            lax.fori_loop(0, count, fill, 0)
            return total + count
        total = lax.fori_loop(0, N_EXPERTS, per_expert, 0)

        def block_copy(blk):
            return pltpu.make_async_copy(zero_buf, xs_hbm.at[pl.ds(pl.multiple_of(blk * tm, tm), tm), :],
                                         blk_sem)

        def fill_block(blk, carry):
            block_copy(blk).start()
            return carry
        lax.fori_loop(nused_ref[0], n_blocks, fill_block, 0)

        def retire(r, carry):
            pltpu.make_async_copy(zero_buf.at[pl.ds(0, 1), :], xs_hbm.at[pl.ds(0, 1), :], pad_sem).wait()
            return carry
        lax.fori_loop(0, total, retire, 0)

        def retire_block(blk, carry):
            block_copy(blk).wait()
            return carry
        lax.fori_loop(nused_ref[0], n_blocks, retire_block, 0)


def _dispatch(dest_flat, pad_start, pad_len, nused, h2, n_slots, tm, td=1024):
    N, half = h2.shape
    grid_spec = pltpu.PrefetchScalarGridSpec(
        num_scalar_prefetch=4,
        grid=(N // td,),
        in_specs=[pl.BlockSpec((td, half), lambda i, d, ps, pn, nu: (i, 0))],
        out_specs=pl.BlockSpec(memory_space=pl.ANY),
        scratch_shapes=[pltpu.VMEM((tm, half), jnp.uint32), pltpu.SemaphoreType.DMA(()),
                        pltpu.SemaphoreType.DMA(()), pltpu.SemaphoreType.DMA(())],
    )
    return pl.pallas_call(
        functools.partial(_dispatch_kernel, td=td, tm=tm),
        grid_spec=grid_spec,
        out_shape=jax.ShapeDtypeStruct((n_slots, half), jnp.uint32),
        compiler_params=_params(("arbitrary",)),
        name="moe_dispatch",
    )(dest_flat, pad_start, pad_len, nused, h2)


def _expert_kernel(be_ref, nused_ref, x_ref, wg_ref, wu_ref, wd_ref, bg_ref, bu_ref, bd_ref, y_ref):
    b = pl.program_id(0)

    @pl.when(b < nused_ref[0])
    def _():
        lo, hi = _unpack_halves(x_ref[...])
        x = jnp.concatenate([lo.astype(BF16), hi.astype(BF16)], axis=1)
        gate = jnp.dot(x, wg_ref[0], preferred_element_type=F32) + bg_ref[0]
        up = jnp.dot(x, wu_ref[0], preferred_element_type=F32) + bu_ref[0]
        gate = jnp.minimum(gate, SWIGLU_LIMIT)
        up = jnp.clip(up, -SWIGLU_LIMIT, SWIGLU_LIMIT)
        act = (up + 1.0) * gate * jax.nn.sigmoid(SWIGLU_ALPHA * gate)
        y = jnp.dot(act.astype(BF16), wd_ref[0], preferred_element_type=F32) + bd_ref[0]
        y_ref[...] = _pack_halves(y)

    @pl.when(b >= nused_ref[0])
    def _():
        y_ref[...] = jnp.zeros_like(y_ref)


def _experts(xs, block_e, nused, wg, wu, wd, bg, bu, bd, tm):
    half = xs.shape[1]
    D = 2 * half
    n_blocks = block_e.shape[0]
    d_ff = wg.shape[2]
    e_map = lambda b, be, nu: (be[b], 0, 0)
    x_map = lambda b, be, nu: (jnp.minimum(b, nu[0] - 1), 0)
    grid_spec = pltpu.PrefetchScalarGridSpec(
        num_scalar_prefetch=2,
        grid=(n_blocks,),
        in_specs=[pl.BlockSpec((tm, half), x_map),
                  pl.BlockSpec((1, D, d_ff), e_map),
                  pl.BlockSpec((1, D, d_ff), e_map),
                  pl.BlockSpec((1, d_ff, D), e_map),
                  pl.BlockSpec((1, 1, d_ff), e_map),
                  pl.BlockSpec((1, 1, d_ff), e_map),
                  pl.BlockSpec((1, 1, D), e_map)],
        out_specs=pl.BlockSpec((tm, half), lambda b, be, nu: (b, 0)),
    )
    return pl.pallas_call(
        _expert_kernel,
        grid_spec=grid_spec,
        out_shape=jax.ShapeDtypeStruct((n_blocks * tm, half), jnp.uint32),
        compiler_params=_params(("arbitrary",)),
        name="experts",
    )(block_e, nused, xs, wg, wu, wd, bg, bu, bd)


SUBLANES = 8


def _combine_kernel(dest_ref, y_hbm, x_ref, gate_ref, mb_ref, ml_ref, o_ref, buf0, buf1, sem, *, tc):
    i = pl.program_id(0)
    n = pl.num_programs(0)
    bufs = (buf0, buf1)
    half = buf0.shape[2]

    def row_copy(d, slot, k, r):
        return pltpu.make_async_copy(y_hbm.at[pl.ds(d, 1), :],
                                     bufs[slot].at[k, pl.ds(r, 1), :], sem.at[slot])

    def issue_rows(step, slot, r0, count):
        for rr in range(count):
            base = (step * tc + r0 + rr) * TOP_K
            for k in range(TOP_K):
                row_copy(dest_ref[base + k], slot, k, r0 + rr).start(priority=k % 2)

    def wait_step(slot):
        for k in range(TOP_K):
            pltpu.make_async_copy(y_hbm.at[pl.ds(0, tc), :], bufs[slot].at[k], sem.at[slot]).wait()

    @pl.when(i == 0)
    def _():
        def body(c, carry):
            issue_rows(0, 0, c * SUBLANES, SUBLANES)
            return carry
        lax.fori_loop(0, tc // SUBLANES, body, 0)

    nxt = jnp.minimum(i + 1, n - 1)
    gate_f = _mod_row(mb_ref, ml_ref, 5)

    def step(slot):
        wait_step(slot)

        def body(c, carry):
            r0 = pl.multiple_of(c * SUBLANES, SUBLANES)
            issue_rows(nxt, 1 - slot, r0, SUBLANES)
            rows = pl.ds(r0, SUBLANES)
            acc_lo = jnp.zeros((SUBLANES, half), F32)
            acc_hi = jnp.zeros((SUBLANES, half), F32)
            for k in range(TOP_K):
                lo, hi = _unpack_halves(bufs[slot][k, rows, :])
                g = gate_ref[rows, k:k + 1]
                acc_lo = acc_lo + g * lo
                acc_hi = acc_hi + g * hi
            o_ref[rows, :half] = x_ref[rows, :half] + gate_f[:, :half] * acc_lo
            o_ref[rows, half:] = x_ref[rows, half:] + gate_f[:, half:] * acc_hi
            return carry
        lax.fori_loop(0, tc // SUBLANES, body, 0)

        @pl.when(i == n - 1)
        def _():
            wait_step(1 - slot)

    for slot in range(2):
        @pl.when(i % 2 == slot)
        def _(slot=slot):
            step(slot)


def _combine(dest_flat, y, x2, gates, mod_base, mod_l, S, tc=256):
    N, D = x2.shape
    per_b = S // tc
    grid_spec = pltpu.PrefetchScalarGridSpec(
        num_scalar_prefetch=1,
        grid=(N // tc,),
        in_specs=[pl.BlockSpec(memory_space=pl.ANY),
                  pl.BlockSpec((tc, D), lambda i, d: (i, 0)),
                  pl.BlockSpec((tc, LANES), lambda i, d: (i, 0)),
                  pl.BlockSpec((1, N_MOD, D), lambda i, d: (i // per_b, 0, 0)),
                  pl.BlockSpec((N_MOD, D), lambda i, d: (0, 0))],
        out_specs=pl.BlockSpec((tc, D), lambda i, d: (i, 0)),
        scratch_shapes=[pltpu.VMEM((TOP_K, tc, D // 2), jnp.uint32),
                        pltpu.VMEM((TOP_K, tc, D // 2), jnp.uint32), pltpu.SemaphoreType.DMA((2,))],
    )
    return pl.pallas_call(
        functools.partial(_combine_kernel, tc=tc),
        grid_spec=grid_spec,
        out_shape=jax.ShapeDtypeStruct((N, D), F32),
        compiler_params=_params(("arbitrary",)),
        name="moe_combine",
    )(dest_flat, y, x2, gates, mod_base, mod_l)


EXPERT_ROWS = 256


def _rope_tables(positions):
    half = ROT_DIM // 2
    inv_freq = ROPE_THETA ** (-jnp.arange(half, dtype=F32) * 2.0 / ROT_DIM)
    ang = positions.astype(F32).reshape(-1, 1) * inv_freq
    cos, sin = jnp.cos(ang), jnp.sin(ang)
    n = ang.shape[0]
    ones = jnp.ones((n, HEAD_DIM - ROT_DIM), F32)
    zeros = jnp.zeros((n, HEAD_DIM - half), F32)
    cos_t = jnp.concatenate([cos, cos, ones], axis=1)
    s1_t = jnp.concatenate([-sin, zeros], axis=1)
    s2_t = jnp.concatenate([jnp.zeros((n, half), F32), sin,
                            jnp.zeros((n, HEAD_DIM - ROT_DIM), F32)], axis=1)
    return cos_t, s1_t, s2_t


def _routing_tables(top_idx, rank, counts, tm, n_blocks):
    N = top_idx.shape[0]
    nblk = (counts + tm - 1) // tm
    blk_end = jnp.cumsum(nblk)
    blk_start = blk_end - nblk
    dest = blk_start[top_idx] * tm + rank
    owner = jnp.sum((blk_end[None, :] <= jnp.arange(n_blocks, dtype=I32)[:, None]).astype(I32), axis=1)
    block_e = jnp.minimum(owner, N_EXPERTS - 1).astype(I32)
    nused = blk_end[-1:].astype(I32)
    pad_start = (blk_start * tm + counts).astype(I32)
    pad_len = (nblk * tm - counts).astype(I32)
    return dest.reshape(-1).astype(I32), block_e, nused, pad_start, pad_len


def kernel(x, c, positions, w_mod, b_mod, mod_layer, w_in, q_norm, k_norm, w_s, b_s, v_ln_g, v_ln_b,
           out_norm, w_out, w_router, b_router, w_gate_up, b_gate_up, w_down, b_down):
    B, S, D = x.shape
    N = B * S
    depth = w_in.shape[0]
    d_ff = w_down.shape[2]
    tm_e = EXPERT_ROWS
    n_blocks = N * TOP_K // tm_e + N_EXPERTS

    mod_base = _mod_base(c, w_mod, b_mod).reshape(B, N_MOD, D)
    cos_t, s1_t, s2_t = _rope_tables(positions)
    x2 = x.reshape(N, D)

    w_in_bf = None
    for l in range(depth):
        mod_l = mod_layer[l]
        h = _mixer_norm(x2, mod_base, mod_l, S)
        qk_gain = jnp.stack([q_norm[l] * HEAD_DIM ** -0.5, k_norm[l]]).reshape(2, 1, HEAD_DIM)
        if w_in_bf is None:
            w_in_bf = _to_bf16(w_in, l, 256)
        qk, v, u, gv, wg, wu = _in_proj(h, w_in_bf, qk_gain, cos_t, s1_t, s2_t,
                                        v_ln_g[l].reshape(1, D_GMLP), v_ln_b[l].reshape(1, D_GMLP),
                                        w_gate_up.reshape(depth, N_EXPERTS * D, 2 * d_ff), l)
        casts = [(w_out, l)] + ([(w_in, l + 1)] if l + 1 < depth else [])
        attn, cast_res = _attention(qk.reshape(B, S, -1), v.reshape(B, S, -1), casts)
        w_out_bf = cast_res[0] if cast_res[0] is not None else _to_bf16(w_out, l, 512)
        w_in_bf = cast_res[1] if l + 1 < depth else None
        mix = _gmlp_and_norms(attn.reshape(N, D_ATTN), u, gv, w_s[l], jnp.transpose(b_s[l]),
                              out_norm[l].reshape(1, -1))
        w_down3 = w_down.reshape(depth, N_EXPERTS * d_ff, D)
        x2, wd = _out_proj(mix, w_out_bf, x2, mod_base, mod_l, S, w_down3, l)

        w_router_p = jnp.pad(w_router[l], ((0, 0), (0, LANES - N_EXPERTS)))
        b_router_p = jnp.pad(b_router[l], (0, LANES - N_EXPERTS)).reshape(1, LANES)
        h2, top_idx, gates, rank, counts = _ffn_norm_router(x2, mod_base, mod_l, w_router_p,
                                                            b_router_p, S)
        dest, block_e, nused, pad_start, pad_len = _routing_tables(
            top_idx[:, :TOP_K], rank[:, :TOP_K], counts[0, :N_EXPERTS], tm_e, n_blocks)
        xs = _dispatch(dest, pad_start, pad_len, nused, h2, n_blocks * tm_e, tm_e)
        if wg is None:
            wg, wu = _split_gate_up(w_gate_up, l)
        if wd is None:
            wd = _to_bf16(w_down3, l, 512)
        wg = wg.reshape(N_EXPERTS, D, d_ff)
        wu = wu.reshape(N_EXPERTS, D, d_ff)
        wd = wd.reshape(N_EXPERTS, d_ff, D)
        bg = b_gate_up[l][:, 0::2].reshape(N_EXPERTS, 1, d_ff)
        bu = b_gate_up[l][:, 1::2].reshape(N_EXPERTS, 1, d_ff)
        y = _experts(xs, block_e, nused, wg, wu, wd, bg, bu,
                     b_down[l].reshape(N_EXPERTS, 1, D), tm_e)
        x2 = _combine(dest, y, x2, gates, mod_base, mod_l, S)
    return x2.reshape(B, S, D)
```

```python
import functools
import math

import jax
import jax.numpy as jnp
from jax import lax
from jax.experimental import pallas as pl
from jax.experimental.pallas import tpu as pltpu

F32 = jnp.float32
BF16 = jnp.bfloat16
I32 = jnp.int32

HEAD_DIM = 128
N_HEADS_ATTN = 16
N_GROUPS_GMLP = 16
D_ATTN = N_HEADS_ATTN * HEAD_DIM
D_GMLP = N_GROUPS_GMLP * HEAD_DIM
ROT_DIM = HEAD_DIM // 4
ROPE_THETA = 500000.0
DILATED_PATTERNS = ((128, 1), (512, 4), (2048, 16))
GMLP_CHUNK = 128
N_EXPERTS = 32
TOP_K = 4
SWIGLU_LIMIT = 7.0
SWIGLU_ALPHA = 1.702
N_MOD = 6
EPS = 1e-6
LANES = 128
MASKED = -1e30

VMEM_LIMIT = 56 * 1024 * 1024


def _params(sem, vmem=VMEM_LIMIT):
    return pltpu.CompilerParams(dimension_semantics=sem, vmem_limit_bytes=vmem)


def _mod_kernel(c_ref, w_ref, b_ref, o_ref):
    c = c_ref[...]
    s = c * jax.nn.sigmoid(c)
    o_ref[...] = jnp.dot(s.astype(BF16), w_ref[...].astype(BF16),
                         preferred_element_type=F32) + b_ref[...]


def _mod_base(c, w_mod, b_mod, tn=512):
    B, D = c.shape
    n_out = w_mod.shape[1]
    return pl.pallas_call(
        _mod_kernel,
        grid=(n_out // tn,),
        in_specs=[pl.BlockSpec((B, D), lambda j: (0, 0)),
                  pl.BlockSpec((D, tn), lambda j: (0, j)),
                  pl.BlockSpec((1, tn), lambda j: (0, j))],
        out_specs=pl.BlockSpec((B, tn), lambda j: (0, j)),
        out_shape=jax.ShapeDtypeStruct((B, n_out), F32),
        compiler_params=_params(("parallel",)),
        name="mod_base",
    )(c, w_mod, b_mod.reshape(1, n_out))


def _pack_halves(x):
    c = x.shape[1] // 2
    bits = pltpu.bitcast(x.astype(BF16).astype(F32), jnp.uint32)
    return (bits[:, c:] & jnp.uint32(0xFFFF0000)) | (bits[:, :c] >> 16)


def _unpack_halves(p):
    lo = pltpu.bitcast(p << 16, F32)
    hi = pltpu.bitcast(p & jnp.uint32(0xFFFF0000), F32)
    return lo, hi


def _mod_row(mb_ref, ml_ref, idx):
    return mb_ref[0, idx:idx + 1, :] + ml_ref[idx:idx + 1, :]


def _norm_mod(x, mb_ref, ml_ref, shift_idx, scale_idx):
    y = x * lax.rsqrt(jnp.mean(x * x, axis=-1, keepdims=True) + EPS)
    return y * (1.0 + _mod_row(mb_ref, ml_ref, scale_idx)) + _mod_row(mb_ref, ml_ref, shift_idx)


def _norm_kernel(x_ref, mb_ref, ml_ref, h_ref):
    h_ref[...] = _norm_mod(x_ref[...], mb_ref, ml_ref, 0, 1).astype(h_ref.dtype)


def _mixer_norm(x2, mod_base, mod_l, S, tm=256):
    N, D = x2.shape
    per_b = S // tm
    return pl.pallas_call(
        _norm_kernel,
        grid=(N // tm,),
        in_specs=[pl.BlockSpec((tm, D), lambda i: (i, 0)),
                  pl.BlockSpec((1, N_MOD, D), lambda i: (i // per_b, 0, 0)),
                  pl.BlockSpec((N_MOD, D), lambda i: (0, 0))],
        out_specs=pl.BlockSpec((tm, D), lambda i: (i, 0)),
        out_shape=jax.ShapeDtypeStruct((N, D), BF16),
        compiler_params=_params(("parallel",)),
        name="mixer_norm",
    )(x2, mod_base, mod_l)


def _gelu(x):
    return 0.5 * x * (1.0 + lax.erf(x * (1.0 / math.sqrt(2.0))))


def _proj_kernel(*refs, mode, sub):
    if mode == "gelu_ln":
        h_ref, w_ref, lg_ref, lb_ref, o_ref = refs
    else:
        h_ref, w_ref, o_ref = refs
    bn = o_ref.shape[1]
    for s in range(bn // sub):
        acc = jnp.dot(h_ref[...], w_ref[:, s * sub:(s + 1) * sub], preferred_element_type=F32)
        for hh in range(sub // HEAD_DIM):
            t = acc[:, hh * HEAD_DIM:(hh + 1) * HEAD_DIM]
            sl = slice(s * sub + hh * HEAD_DIM, s * sub + (hh + 1) * HEAD_DIM)
            if mode == "gelu":
                r = _gelu(t)
            elif mode == "gelu_ln":
                g = _gelu(t)
                gc = g - jnp.mean(g, axis=-1, keepdims=True)
                r = gc * lax.rsqrt(jnp.mean(gc * gc, axis=-1, keepdims=True) + EPS)
                r = r * lg_ref[:, sl] + lb_ref[:, sl]
            else:
                r = t
            o_ref[:, sl] = r.astype(o_ref.dtype)


def _proj(h, w, col0, width, mode, extra=(), extra_specs=(), bm=1024, bn=1024, sub=256):
    N, D = h.shape
    j0 = col0 // bn
    return pl.pallas_call(
        functools.partial(_proj_kernel, mode=mode, sub=sub),
        grid=(N // bm, width // bn),
        in_specs=[pl.BlockSpec((bm, D), lambda i, j: (i, 0)),
                  pl.BlockSpec((D, bn), lambda i, j: (0, j0 + j))] + list(extra_specs),
        out_specs=pl.BlockSpec((bm, bn), lambda i, j: (i, j)),
        out_shape=jax.ShapeDtypeStruct((N, width), BF16),
        compiler_params=_params(("parallel", "arbitrary")),
        name="in_proj_" + mode,
    )(h, w, *extra)


MXU_DIM = 256


def _deinterleave_gate_up(w2, g2, u2):
    r = lax.broadcasted_iota(I32, (MXU_DIM, MXU_DIM), 0)
    c = lax.broadcasted_iota(I32, (MXU_DIM, MXU_DIM), 1)
    src = jnp.where(c < LANES, 2 * c, 2 * (c - LANES) + 1)
    perm = jnp.where(r == src, 1.0, 0.0).astype(BF16)
    for ch in range(w2.shape[1] // MXU_DIM):
        w = w2[:, ch * MXU_DIM:(ch + 1) * MXU_DIM].astype(BF16)
        res = jnp.dot(w, perm, preferred_element_type=F32)
        g2[:, ch * LANES:(ch + 1) * LANES] = res[:, :LANES].astype(BF16)
        u2[:, ch * LANES:(ch + 1) * LANES] = res[:, LANES:].astype(BF16)


def _qk_proj_kernel(h_ref, w_ref, gain_ref, cos_ref, s1_ref, s2_ref, *rest):
    if len(rest) == 6:
        side_ref, o_ref, g_ref, u_ref, acc0, acc1 = rest
    else:
        side_ref = None
        o_ref, acc0, acc1 = rest
    t = pl.program_id(0)
    bn = o_ref.shape[1]

    @pl.when(t == 0)
    def _():
        acc1[...] = jnp.zeros_like(acc1)

    def step(acc_mm, acc_ep):
        acc_mm[...] = jnp.dot(h_ref[...], w_ref[...], preferred_element_type=F32)
        if side_ref is not None:
            _deinterleave_gate_up(side_ref.at[0], g_ref, u_ref)
        for hh in range(bn // HEAD_DIM):
            sl = slice(hh * HEAD_DIM, (hh + 1) * HEAD_DIM)
            a = acc_ep[:, sl]
            y = a * lax.rsqrt(jnp.mean(a * a, axis=-1, keepdims=True) + EPS) * gain_ref[0]
            r = (y * cos_ref[...] + pltpu.roll(y, HEAD_DIM - ROT_DIM // 2, 1) * s1_ref[...]
                 + pltpu.roll(y, ROT_DIM // 2, 1) * s2_ref[...])
            o_ref[:, sl] = r.astype(o_ref.dtype)

    @pl.when(t % 2 == 0)
    def _():
        step(acc0, acc1)

    @pl.when(t % 2 == 1)
    def _():
        step(acc1, acc0)


def _qk_proj(h, w, qk_gain, cos_t, s1_t, s2_t, side3, layer, bm=1024, bn=512):
    N, D = h.shape
    width = 2 * D_ATTN
    nj = width // bn
    n_tiles = (N // bm) * nj
    tiles_per_part = D_ATTN // bn
    mm = lambda t: jnp.minimum(t, n_tiles - 1)
    ep = lambda t: jnp.maximum(t - 1, 0)
    tok_tab = pl.BlockSpec((bm, HEAD_DIM), lambda t: (ep(t) // nj, 0))
    _, side_rows, side_cols = side3.shape
    rows = _rider_rows(side_rows, side_cols, n_tiles)
    in_specs = [pl.BlockSpec((bm, D), lambda t: (mm(t) // nj, 0)),
                pl.BlockSpec((D, bn), lambda t: (0, mm(t) % nj)),
                pl.BlockSpec((1, 1, HEAD_DIM), lambda t: ((ep(t) % nj) // tiles_per_part, 0, 0)),
                tok_tab, tok_tab, tok_tab]
    out_specs = [pl.BlockSpec((bm, bn), lambda t: (ep(t) // nj, ep(t) % nj))]
    out_shape = [jax.ShapeDtypeStruct((N, width), BF16)]
    args = [h, w, qk_gain, cos_t, s1_t, s2_t]
    if rows is not None:
        half_spec = pl.BlockSpec((rows, side_cols // 2), lambda t: (mm(t), 0))
        in_specs.append(pl.BlockSpec((1, rows, side_cols), lambda t: (layer, mm(t), 0)))
        out_specs += [half_spec, half_spec]
        out_shape += [jax.ShapeDtypeStruct((side_rows, side_cols // 2), BF16)] * 2
        args.append(side3)
    res = pl.pallas_call(
        _qk_proj_kernel,
        grid=(n_tiles + 1,),
        in_specs=in_specs,
        out_specs=out_specs,
        out_shape=out_shape,
        scratch_shapes=[pltpu.VMEM((bm, bn), F32), pltpu.VMEM((bm, bn), F32)],
        compiler_params=_params(("arbitrary",)),
        name="in_proj_qk",
    )(*args)
    return (res[0], res[1], res[2]) if rows is not None else (res[0], None, None)


def _cast_kernel(x_ref, o_ref):
    o_ref[...] = x_ref[0].astype(o_ref.dtype)


def _to_bf16(w3, layer, tr):
    _, R, C = w3.shape
    return pl.pallas_call(
        _cast_kernel,
        grid=(R // tr,),
        in_specs=[pl.BlockSpec((1, tr, C), lambda i: (layer, i, 0))],
        out_specs=pl.BlockSpec((tr, C), lambda i: (i, 0)),
        out_shape=jax.ShapeDtypeStruct((R, C), BF16),
        compiler_params=_params(("parallel",)),
        name="to_bf16",
    )(w3)


def _in_proj(h, w_in, qk_gain, cos_t, s1_t, s2_t, ln_g, ln_b, gate_up3, layer, bm=1024, bn=1024):
    qk, wg, wu = _qk_proj(h, w_in, qk_gain, cos_t, s1_t, s2_t, gate_up3, layer)
    v = _proj(h, w_in, 2 * D_ATTN, D_ATTN, "none", bm=bm, bn=bn)
    u = _proj(h, w_in, 3 * D_ATTN, D_GMLP, "gelu", bm=bm, bn=bn)
    row = pl.BlockSpec((1, bn), lambda i, j: (0, j))
    gv = _proj(h, w_in, 3 * D_ATTN + D_GMLP, D_GMLP, "gelu_ln", (ln_g, ln_b), (row, row), bm, bn)
    return qk, v, u, gv, wg, wu


def _attn_kernel(q_ref, k_ref, v_ref, bias_ref, *rest, tq):
    n_cast = (len(rest) - 1) // 2
    o_ref = rest[n_cast]
    for src, dst in zip(rest[:n_cast], rest[n_cast + 1:]):
        dst[...] = src[0].astype(dst.dtype)
    S = q_ref.shape[1]
    n_q = S // tq
    for i in range(n_q):
        kend = (i + 1) * tq
        off = (n_q - 1 - i) * tq
        for hh in range(q_ref.shape[2] // HEAD_DIM):
            cols = slice(hh * HEAD_DIM, (hh + 1) * HEAD_DIM)
            q = q_ref[0, i * tq:(i + 1) * tq, cols]
            s = lax.dot_general(q, k_ref[0, :kend, cols], (((1,), (1,)), ((), ())),
                                preferred_element_type=F32)
            s = s + bias_ref[:, off:off + kend]
            m = jnp.max(s, axis=-1, keepdims=True)
            p = jnp.exp(s - m)
            l = jnp.sum(p, axis=-1, keepdims=True)
            o = jnp.dot(p.astype(BF16), v_ref[0, :kend, cols], preferred_element_type=F32)
            o_ref[0, i * tq:(i + 1) * tq, cols] = (o / l).astype(o_ref.dtype)


def _attn_bias_table(S, tq):
    r = jnp.arange(tq, dtype=I32)[:, None]
    u = jnp.arange(S, dtype=I32)[None, :]
    d = r + (S - tq) - u
    mult = jnp.zeros((tq, S), F32)
    for window, dil in DILATED_PATTERNS:
        mult = mult + ((d >= 0) & (d % dil == 0) & (d <= window)).astype(F32)
    return jnp.where(mult > 0, jnp.log(jnp.maximum(mult, 1.0)), MASKED)


def _attention(qk3, v3, casts=(), tq=256, heads_per_step=2):
    B, S, _ = v3.shape
    bias = _attn_bias_table(S, tq)
    H = N_HEADS_ATTN // heads_per_step
    width = heads_per_step * HEAD_DIM
    head_blk = pl.BlockSpec((1, S, width), lambda b, h: (b, 0, h))
    in_specs = [head_blk, pl.BlockSpec((1, S, width), lambda b, h: (b, 0, H + h)), head_blk,
                pl.BlockSpec((tq, S), lambda b, h: (0, 0))]
    out_specs = [head_blk]
    out_shape = [jax.ShapeDtypeStruct((B, S, D_ATTN), BF16)]
    args = [qk3, qk3, v3, bias]
    riding = []
    for arr3, layer in casts:
        _, total_rows, cols = arr3.shape
        rows = _rider_rows(total_rows, cols, B * H)
        riding.append(rows is not None)
        if rows is not None:
            in_specs.append(pl.BlockSpec((1, rows, cols), lambda b, h, layer=layer: (layer, b * H + h, 0)))
            out_specs.append(pl.BlockSpec((rows, cols), lambda b, h: (b * H + h, 0)))
            out_shape.append(jax.ShapeDtypeStruct((total_rows, cols), BF16))
            args.append(arr3)
    res = pl.pallas_call(
        functools.partial(_attn_kernel, tq=tq),
        grid=(B, H),
        in_specs=in_specs,
        out_specs=out_specs,
        out_shape=out_shape,
        compiler_params=_params(("parallel", "parallel")),
        name="dilated_attn",
    )(*args)
    outs = iter(res[1:])
    return res[0], [next(outs) if r else None for r in riding]


def _gmlp_kernel(a_ref, u_ref, gv_ref, ws_ref, bst_ref, on_ref, o_ref, g_scr):
    tg = a_ref.shape[0]
    row = lax.broadcasted_iota(I32, (GMLP_CHUNK, GMLP_CHUNK), 0)
    col = lax.broadcasted_iota(I32, (GMLP_CHUNK, GMLP_CHUNK), 1)
    causal = row >= col
    for g in range(N_GROUPS_GMLP):
        w = jnp.where(causal, ws_ref[g], 0.0).astype(BF16)
        bias = bst_ref[:, g:g + 1]
        sl = slice(g * HEAD_DIM, (g + 1) * HEAD_DIM)
        for c in range(tg // GMLP_CHUNK):
            rows = slice(c * GMLP_CHUNK, (c + 1) * GMLP_CHUNK)
            mixed = jnp.dot(w, gv_ref[rows, sl], preferred_element_type=F32) + bias
            g_scr[rows, sl] = u_ref[rows, sl].astype(F32) * mixed
    gm = g_scr[...]
    gm = gm * lax.rsqrt(jnp.mean(gm * gm, axis=-1, keepdims=True) + EPS)
    o_ref[:, D_ATTN:] = (gm * on_ref[:, D_ATTN:]).astype(o_ref.dtype)
    a = a_ref[...].astype(F32)
    a = a * lax.rsqrt(jnp.mean(a * a, axis=-1, keepdims=True) + EPS)
    o_ref[:, :D_ATTN] = (a * on_ref[:, :D_ATTN]).astype(o_ref.dtype)


def _gmlp_and_norms(attn2, u, gv, w_s, b_s_t, out_norm, tg=512):
    N = attn2.shape[0]
    D_mix = D_ATTN + D_GMLP
    return pl.pallas_call(
        _gmlp_kernel,
        grid=(N // tg,),
        in_specs=[pl.BlockSpec((tg, D_ATTN), lambda i: (i, 0)),
                  pl.BlockSpec((tg, D_GMLP), lambda i: (i, 0)),
                  pl.BlockSpec((tg, D_GMLP), lambda i: (i, 0)),
                  pl.BlockSpec((N_GROUPS_GMLP, GMLP_CHUNK, GMLP_CHUNK), lambda i: (0, 0, 0)),
                  pl.BlockSpec((GMLP_CHUNK, N_GROUPS_GMLP), lambda i: (0, 0)),
                  pl.BlockSpec((1, D_mix), lambda i: (0, 0))],
        out_specs=pl.BlockSpec((tg, D_mix), lambda i: (i, 0)),
        out_shape=jax.ShapeDtypeStruct((N, D_mix), BF16),
        scratch_shapes=[pltpu.VMEM((tg, D_GMLP), F32)],
        compiler_params=_params(("parallel",)),
        name="gmlp_norms",
    )(attn2, u, gv, w_s, b_s_t, out_norm)


RIDER_MAX_BYTES = 8 * 1024 * 1024


def _rider_rows(total_rows, cols, steps):
    if total_rows % steps:
        return None
    rows = total_rows // steps
    if rows % 16 or rows * cols * 4 > RIDER_MAX_BYTES:
        return None
    return rows


def _outproj_kernel(m_ref, w_ref, x_ref, mb_ref, ml_ref, *rest):
    if len(rest) == 3:
        side_ref, o_ref, side_out = rest
        side_out[...] = side_ref[0].astype(side_out.dtype)
    else:
        o_ref, = rest
    acc = jnp.dot(m_ref[...], w_ref[...], preferred_element_type=F32)
    o_ref[...] = x_ref[...] + _mod_row(mb_ref, ml_ref, 2) * acc


def _out_proj(mix, w_out, x2, mod_base, mod_l, S, side3, layer, bm=1024, bn=512):
    N, D_mix = mix.shape
    D = w_out.shape[1]
    per_b = S // bm
    nj = D // bn
    rows = None
    if side3 is not None:
        _, side_rows, side_cols = side3.shape
        rows = _rider_rows(side_rows, side_cols, (N // bm) * nj)
    in_specs = [pl.BlockSpec((bm, D_mix), lambda i, j: (i, 0)),
                pl.BlockSpec((D_mix, bn), lambda i, j: (0, j)),
                pl.BlockSpec((bm, bn), lambda i, j: (i, j)),
                pl.BlockSpec((1, N_MOD, bn), lambda i, j: (i // per_b, 0, j)),
                pl.BlockSpec((N_MOD, bn), lambda i, j: (0, j))]
    out_specs = [pl.BlockSpec((bm, bn), lambda i, j: (i, j))]
    out_shape = [jax.ShapeDtypeStruct((N, D), F32)]
    args = [mix, w_out, x2, mod_base, mod_l]
    if rows is not None:
        in_specs.append(pl.BlockSpec((1, rows, side_cols), lambda i, j: (layer, i * nj + j, 0)))
        out_specs.append(pl.BlockSpec((rows, side_cols), lambda i, j: (i * nj + j, 0)))
        out_shape.append(jax.ShapeDtypeStruct((side_rows, side_cols), BF16))
        args.append(side3)
    res = pl.pallas_call(
        _outproj_kernel,
        grid=(N // bm, nj),
        in_specs=in_specs,
        out_specs=out_specs,
        out_shape=out_shape,
        compiler_params=_params(("parallel", "arbitrary")),
        name="out_proj",
    )(*args)
    return (res[0], res[1]) if rows is not None else (res[0], None)


def _router_kernel(x_ref, mb_ref, ml_ref, wr_ref, br_ref,
                   h_ref, idx_ref, gate_ref, rank_ref, cnt_ref, cnt_scr):
    i = pl.program_id(0)
    tm = x_ref.shape[0]

    @pl.when(i == 0)
    def _():
        cnt_scr[...] = jnp.zeros_like(cnt_scr)

    h = _norm_mod(x_ref[...], mb_ref, ml_ref, 3, 4)
    h_ref[...] = _pack_halves(h)
    logits = jnp.dot(h.astype(BF16), wr_ref[...].astype(BF16),
                     preferred_element_type=F32) + br_ref[...]
    lane = lax.broadcasted_iota(I32, (tm, LANES), 1)
    lane_f = lane.astype(F32)
    vals = jnp.where(lane < N_EXPERTS, logits, -jnp.inf)
    tops, sels = [], []
    idx_out = jnp.zeros((tm, LANES), F32)
    for k in range(TOP_K):
        m = jnp.max(vals, axis=-1, keepdims=True)
        first = jnp.min(jnp.where(vals == m, lane_f, float(LANES)), axis=-1, keepdims=True)
        sel = lane_f == first
        tops.append(m)
        sels.append(sel)
        idx_out = jnp.where(lane == k, first, idx_out)
        vals = jnp.where(sel, -jnp.inf, vals)
    exps = [jnp.exp(t - tops[0]) for t in tops]
    denom = exps[0] + exps[1] + exps[2] + exps[3]
    gate_out = jnp.zeros((tm, LANES), F32)
    for k in range(TOP_K):
        gate_out = jnp.where(lane == k, exps[k] / denom, gate_out)

    chosen = jnp.zeros((tm, LANES), F32)
    for sel in sels:
        chosen = chosen + sel.astype(F32)
    r = lax.broadcasted_iota(I32, (tm, tm), 0)
    c = lax.broadcasted_iota(I32, (tm, tm), 1)
    earlier = jnp.where(r > c, 1.0, 0.0).astype(BF16)
    before = jnp.dot(earlier, chosen.astype(BF16), preferred_element_type=F32) + cnt_scr[...]
    rank_out = jnp.zeros((tm, LANES), F32)
    for k in range(TOP_K):
        rk = jnp.sum(jnp.where(sels[k], before, 0.0), axis=-1, keepdims=True)
        rank_out = jnp.where(lane == k, rk, rank_out)
    cnt_scr[...] = cnt_scr[...] + jnp.sum(chosen, axis=0, keepdims=True)

    idx_ref[...] = idx_out.astype(I32)
    gate_ref[...] = gate_out
    rank_ref[...] = rank_out.astype(I32)
    cnt_ref[...] = cnt_scr[...].astype(I32)


def _ffn_norm_router(x2, mod_base, mod_l, w_router_p, b_router_p, S, tm=256):
    N, D = x2.shape
    per_b = S // tm
    tok_lane = pl.BlockSpec((tm, LANES), lambda i: (i, 0))
    return pl.pallas_call(
        _router_kernel,
        grid=(N // tm,),
        in_specs=[pl.BlockSpec((tm, D), lambda i: (i, 0)),
                  pl.BlockSpec((1, N_MOD, D), lambda i: (i // per_b, 0, 0)),
                  pl.BlockSpec((N_MOD, D), lambda i: (0, 0)),
                  pl.BlockSpec((D, LANES), lambda i: (0, 0)),
                  pl.BlockSpec((1, LANES), lambda i: (0, 0))],
        out_specs=[pl.BlockSpec((tm, D // 2), lambda i: (i, 0)), tok_lane, tok_lane, tok_lane,
                   pl.BlockSpec((1, LANES), lambda i: (0, 0))],
        out_shape=[jax.ShapeDtypeStruct((N, D // 2), jnp.uint32),
                   jax.ShapeDtypeStruct((N, LANES), I32),
                   jax.ShapeDtypeStruct((N, LANES), F32),
                   jax.ShapeDtypeStruct((N, LANES), I32),
                   jax.ShapeDtypeStruct((1, LANES), I32)],
        scratch_shapes=[pltpu.VMEM((1, LANES), F32)],
        compiler_params=_params(("arbitrary",)),
        name="ffn_norm_router",
    )(x2, mod_base, mod_l, w_router_p, b_router_p)


def _split_gate_up_kernel(w_ref, g_ref, u_ref):
    _deinterleave_gate_up(w_ref.at[0, 0], g_ref.at[0], u_ref.at[0])


def _split_gate_up(w_gate_up, layer, tr=1024):
    _, E, D, two_ff = w_gate_up.shape
    d_ff = two_ff // 2
    out = jax.ShapeDtypeStruct((E, D, d_ff), BF16)
    return pl.pallas_call(
        _split_gate_up_kernel,
        grid=(E, D // tr),
        in_specs=[pl.BlockSpec((1, 1, tr, two_ff), lambda e, i: (layer, e, i, 0))],
        out_specs=[pl.BlockSpec((1, tr, d_ff), lambda e, i: (e, i, 0)),
                   pl.BlockSpec((1, tr, d_ff), lambda e, i: (e, i, 0))],
        out_shape=[out, out],
        compiler_params=_params(("parallel", "parallel")),
        name="split_gate_up",
    )(w_gate_up)


def _dispatch_kernel(dest_ref, pad_start_ref, pad_len_ref, nused_ref, h_ref, xs_hbm,
                     zero_buf, sem, pad_sem, blk_sem, *, td, tm):
    i = pl.program_id(0)
    n = pl.num_programs(0)
    n_blocks = xs_hbm.shape[0] // tm

    def body(r, carry):
        base = (i * td + r) * TOP_K
        for k in range(TOP_K):
            pltpu.make_async_copy(h_ref.at[pl.ds(r, 1), :], xs_hbm.at[pl.ds(dest_ref[base + k], 1), :],
                                  sem).start(priority=k % 2)
        return carry
    lax.fori_loop(0, td, body, 0, unroll=2)

    for k in range(TOP_K):
        pltpu.make_async_copy(h_ref, xs_hbm.at[pl.ds(0, td), :], sem).wait()

    @pl.when(i == n - 1)
    def _():
        zero_buf[...] = jnp.zeros_like(zero_buf)

        def per_expert(e, total):
            start = pad_start_ref[e]
            count = pad_len_ref[e]

            def fill(r, carry):
                pltpu.make_async_copy(zero_buf.at[pl.ds(0, 1), :], xs_hbm.at[pl.ds(start + r, 1), :],
                                      pad_sem).start()
                return carry
            lax.fori_loop(0, count, fill, 0)
            return total + count
        total = lax.fori_loop(0, N_EXPERTS, per_expert, 0)

        def block_copy(blk):
            return pltpu.make_async_copy(zero_buf, xs_hbm.at[pl.ds(pl.multiple_of(blk * tm, tm), tm), :],
                                         blk_sem)

        def fill_block(blk, carry):
            block_copy(blk).start()
            return carry
        lax.fori_loop(nused_ref[0], n_blocks, fill_block, 0)

        def retire(r, carry):
            pltpu.make_async_copy(zero_buf.at[pl.ds(0, 1), :], xs_hbm.at[pl.ds(0, 1), :], pad_sem).wait()
            return carry
        lax.fori_loop(0, total, retire, 0)

        def retire_block(blk, carry):
            block_copy(blk).wait()
            return carry
        lax.fori_loop(nused_ref[0], n_blocks, retire_block, 0)


def _dispatch(dest_flat, pad_start, pad_len, nused, h2, n_slots, tm, td=1024):
    N, half = h2.shape
    grid_spec = pltpu.PrefetchScalarGridSpec(
        num_scalar_prefetch=4,
        grid=(N // td,),
        in_specs=[pl.BlockSpec((td, half), lambda i, d, ps, pn, nu: (i, 0))],
        out_specs=pl.BlockSpec(memory_space=pl.ANY),
        scratch_shapes=[pltpu.VMEM((tm, half), jnp.uint32), pltpu.SemaphoreType.DMA(()),
                        pltpu.SemaphoreType.DMA(()), pltpu.SemaphoreType.DMA(())],
    )
    return pl.pallas_call(
        functools.partial(_dispatch_kernel, td=td, tm=tm),
        grid_spec=grid_spec,
        out_shape=jax.ShapeDtypeStruct((n_slots, half), jnp.uint32),
        compiler_params=_params(("arbitrary",)),
        name="moe_dispatch",
    )(dest_flat, pad_start, pad_len, nused, h2)


def _expert_kernel(be_ref, nused_ref, x_ref, wg_ref, wu_ref, wd_ref, bg_ref, bu_ref, bd_ref, y_ref):
    b = pl.program_id(0)

    @pl.when(b < nused_ref[0])
    def _():
        lo, hi = _unpack_halves(x_ref[...])
        x = jnp.concatenate([lo.astype(BF16), hi.astype(BF16)], axis=1)
        gate = jnp.dot(x, wg_ref[0], preferred_element_type=F32) + bg_ref[0]
        up = jnp.dot(x, wu_ref[0], preferred_element_type=F32) + bu_ref[0]
        gate = jnp.minimum(gate, SWIGLU_LIMIT)
        up = jnp.clip(up, -SWIGLU_LIMIT, SWIGLU_LIMIT)
        act = (up + 1.0) * gate * jax.nn.sigmoid(SWIGLU_ALPHA * gate)
        y = jnp.dot(act.astype(BF16), wd_ref[0], preferred_element_type=F32) + bd_ref[0]
        y_ref[...] = _pack_halves(y)

    @pl.when(b >= nused_ref[0])
    def _():
        y_ref[...] = jnp.zeros_like(y_ref)


def _experts(xs, block_e, nused, wg, wu, wd, bg, bu, bd, tm):
    half = xs.shape[1]
    D = 2 * half
    n_blocks = block_e.shape[0]
    d_ff = wg.shape[2]
    e_map = lambda b, be, nu: (be[b], 0, 0)
    x_map = lambda b, be, nu: (jnp.minimum(b, nu[0] - 1), 0)
    grid_spec = pltpu.PrefetchScalarGridSpec(
        num_scalar_prefetch=2,
        grid=(n_blocks,),
        in_specs=[pl.BlockSpec((tm, half), x_map),
                  pl.BlockSpec((1, D, d_ff), e_map),
                  pl.BlockSpec((1, D, d_ff), e_map),
                  pl.BlockSpec((1, d_ff, D), e_map),
                  pl.BlockSpec((1, 1, d_ff), e_map),
                  pl.BlockSpec((1, 1, d_ff), e_map),
                  pl.BlockSpec((1, 1, D), e_map)],
        out_specs=pl.BlockSpec((tm, half), lambda b, be, nu: (b, 0)),
    )
    return pl.pallas_call(
        _expert_kernel,
        grid_spec=grid_spec,
        out_shape=jax.ShapeDtypeStruct((n_blocks * tm, half), jnp.uint32),
        compiler_params=_params(("arbitrary",)),
        name="experts",
    )(block_e, nused, xs, wg, wu, wd, bg, bu, bd)


SUBLANES = 8


def _combine_kernel(dest_ref, y_hbm, x_ref, gate_ref, mb_ref, ml_ref, o_ref, buf0, buf1, sem, *, tc):
    i = pl.program_id(0)
    n = pl.num_programs(0)
    bufs = (buf0, buf1)
    half = buf0.shape[2]

    def row_copy(d, slot, k, r):
        return pltpu.make_async_copy(y_hbm.at[pl.ds(d, 1), :],
                                     bufs[slot].at[k, pl.ds(r, 1), :], sem.at[slot])

    def issue_rows(step, slot, r0, count):
        for rr in range(count):
            base = (step * tc + r0 + rr) * TOP_K
            for k in range(TOP_K):
                row_copy(dest_ref[base + k], slot, k, r0 + rr).start(priority=k % 2)

    def wait_step(slot):
        for k in range(TOP_K):
            pltpu.make_async_copy(y_hbm.at[pl.ds(0, tc), :], bufs[slot].at[k], sem.at[slot]).wait()

    @pl.when(i == 0)
    def _():
        def body(c, carry):
            issue_rows(0, 0, c * SUBLANES, SUBLANES)
            return carry
        lax.fori_loop(0, tc // SUBLANES, body, 0)

    nxt = jnp.minimum(i + 1, n - 1)
    gate_f = _mod_row(mb_ref, ml_ref, 5)

    def step(slot):
        wait_step(slot)

        def body(c, carry):
            r0 = pl.multiple_of(c * SUBLANES, SUBLANES)
            issue_rows(nxt, 1 - slot, r0, SUBLANES)
            rows = pl.ds(r0, SUBLANES)
            acc_lo = jnp.zeros((SUBLANES, half), F32)
            acc_hi = jnp.zeros((SUBLANES, half), F32)
            for k in range(TOP_K):
                lo, hi = _unpack_halves(bufs[slot][k, rows, :])
                g = gate_ref[rows, k:k + 1]
                acc_lo = acc_lo + g * lo
                acc_hi = acc_hi + g * hi
            o_ref[rows, :half] = x_ref[rows, :half] + gate_f[:, :half] * acc_lo
            o_ref[rows, half:] = x_ref[rows, half:] + gate_f[:, half:] * acc_hi
            return carry
        lax.fori_loop(0, tc // SUBLANES, body, 0)

        @pl.when(i == n - 1)
        def _():
            wait_step(1 - slot)

    for slot in range(2):
        @pl.when(i % 2 == slot)
        def _(slot=slot):
            step(slot)


def _combine(dest_flat, y, x2, gates, mod_base, mod_l, S, tc=256):
    N, D = x2.shape
    per_b = S // tc
    grid_spec = pltpu.PrefetchScalarGridSpec(
        num_scalar_prefetch=1,
        grid=(N // tc,),
        in_specs=[pl.BlockSpec(memory_space=pl.ANY),
                  pl.BlockSpec((tc, D), lambda i, d: (i, 0)),
                  pl.BlockSpec((tc, LANES), lambda i, d: (i, 0)),
                  pl.BlockSpec((1, N_MOD, D), lambda i, d: (i // per_b, 0, 0)),
                  pl.BlockSpec((N_MOD, D), lambda i, d: (0, 0))],
        out_specs=pl.BlockSpec((tc, D), lambda i, d: (i, 0)),
        scratch_shapes=[pltpu.VMEM((TOP_K, tc, D // 2), jnp.uint32),
                        pltpu.VMEM((TOP_K, tc, D // 2), jnp.uint32), pltpu.SemaphoreType.DMA((2,))],
    )
    return pl.pallas_call(
        functools.partial(_combine_kernel, tc=tc),
        grid_spec=grid_spec,
        out_shape=jax.ShapeDtypeStruct((N, D), F32),
        compiler_params=_params(("arbitrary",)),
        name="moe_combine",
    )(dest_flat, y, x2, gates, mod_base, mod_l)


EXPERT_ROWS = 256


def _rope_tables(positions):
    half = ROT_DIM // 2
    inv_freq = ROPE_THETA ** (-jnp.arange(half, dtype=F32) * 2.0 / ROT_DIM)
    ang = positions.astype(F32).reshape(-1, 1) * inv_freq
    cos, sin = jnp.cos(ang), jnp.sin(ang)
    n = ang.shape[0]
    ones = jnp.ones((n, HEAD_DIM - ROT_DIM), F32)
    zeros = jnp.zeros((n, HEAD_DIM - half), F32)
    cos_t = jnp.concatenate([cos, cos, ones], axis=1)
    s1_t = jnp.concatenate([-sin, zeros], axis=1)
    s2_t = jnp.concatenate([jnp.zeros((n, half), F32), sin,
                            jnp.zeros((n, HEAD_DIM - ROT_DIM), F32)], axis=1)
    return cos_t, s1_t, s2_t


def _routing_tables(top_idx, rank, counts, tm, n_blocks):
    N = top_idx.shape[0]
    nblk = (counts + tm - 1) // tm
    blk_end = jnp.cumsum(nblk)
    blk_start = blk_end - nblk
    dest = blk_start[top_idx] * tm + rank
    owner = jnp.sum((blk_end[None, :] <= jnp.arange(n_blocks, dtype=I32)[:, None]).astype(I32), axis=1)
    block_e = jnp.minimum(owner, N_EXPERTS - 1).astype(I32)
    nused = blk_end[-1:].astype(I32)
    pad_start = (blk_start * tm + counts).astype(I32)
    pad_len = (nblk * tm - counts).astype(I32)
    return dest.reshape(-1).astype(I32), block_e, nused, pad_start, pad_len


def kernel(x, c, positions, w_mod, b_mod, mod_layer, w_in, q_norm, k_norm, w_s, b_s, v_ln_g, v_ln_b,
           out_norm, w_out, w_router, b_router, w_gate_up, b_gate_up, w_down, b_down):
    B, S, D = x.shape
    N = B * S
    depth = w_in.shape[0]
    d_ff = w_down.shape[2]
    tm_e = EXPERT_ROWS
    n_blocks = N * TOP_K // tm_e + N_EXPERTS

    mod_base = _mod_base(c, w_mod, b_mod).reshape(B, N_MOD, D)
    cos_t, s1_t, s2_t = _rope_tables(positions)
    x2 = x.reshape(N, D)

    w_in_bf = None
    for l in range(depth):
        mod_l = mod_layer[l]
        h = _mixer_norm(x2, mod_base, mod_l, S)
        qk_gain = jnp.stack([q_norm[l] * HEAD_DIM ** -0.5, k_norm[l]]).reshape(2, 1, HEAD_DIM)
        if w_in_bf is None:
            w_in_bf = _to_bf16(w_in, l, 256)
        qk, v, u, gv, wg, wu = _in_proj(h, w_in_bf, qk_gain, cos_t, s1_t, s2_t,
                                        v_ln_g[l].reshape(1, D_GMLP), v_ln_b[l].reshape(1, D_GMLP),
                                        w_gate_up.reshape(depth, N_EXPERTS * D, 2 * d_ff), l)
        w_down3 = w_down.reshape(depth, N_EXPERTS * d_ff, D)
        casts = [(w_out, l), (w_down3, l)] + ([(w_in, l + 1)] if l + 1 < depth else [])
        attn, cast_res = _attention(qk.reshape(B, S, -1), v.reshape(B, S, -1), casts)
        w_out_bf = cast_res[0] if cast_res[0] is not None else _to_bf16(w_out, l, 512)
        wd_att = cast_res[1]
        w_in_bf = cast_res[2] if l + 1 < depth else None
        mix = _gmlp_and_norms(attn.reshape(N, D_ATTN), u, gv, w_s[l], jnp.transpose(b_s[l]),
                              out_norm[l].reshape(1, -1))
        x2, wd = _out_proj(mix, w_out_bf, x2, mod_base, mod_l, S, None if wd_att is not None else w_down3, l)
        if wd_att is not None:
            wd = wd_att

        w_router_p = jnp.pad(w_router[l], ((0, 0), (0, LANES - N_EXPERTS)))
        b_router_p = jnp.pad(b_router[l], (0, LANES - N_EXPERTS)).reshape(1, LANES)
        h2, top_idx, gates, rank, counts = _ffn_norm_router(x2, mod_base, mod_l, w_router_p,
                                                            b_router_p, S)
        dest, block_e, nused, pad_start, pad_len = _routing_tables(
            top_idx[:, :TOP_K], rank[:, :TOP_K], counts[0, :N_EXPERTS], tm_e, n_blocks)
        xs = _dispatch(dest, pad_start, pad_len, nused, h2, n_blocks * tm_e, tm_e)
        if wg is None:
            wg, wu = _split_gate_up(w_gate_up, l)
        if wd is None:
            wd = _to_bf16(w_down3, l, 512)
        wg = wg.reshape(N_EXPERTS, D, d_ff)
        wu = wu.reshape(N_EXPERTS, D, d_ff)
        wd = wd.reshape(N_EXPERTS, d_ff, D)
        bg = b_gate_up[l][:, 0::2].reshape(N_EXPERTS, 1, d_ff)
        bu = b_gate_up[l][:, 1::2].reshape(N_EXPERTS, 1, d_ff)
        y = _experts(xs, block_e, nused, wg, wu, wd, bg, bu,
                     b_down[l].reshape(N_EXPERTS, 1, D), tm_e)
        x2 = _combine(dest, y, x2, gates, mod_base, mod_l, S)
    return x2.reshape(B, S, D)
```

```python
import functools
import math

import jax
import jax.numpy as jnp
from jax import lax
from jax.experimental import pallas as pl
from jax.experimental.pallas import tpu as pltpu

F32 = jnp.float32
BF16 = jnp.bfloat16
I32 = jnp.int32

HEAD_DIM = 128
N_HEADS_ATTN = 16
N_GROUPS_GMLP = 16
D_ATTN = N_HEADS_ATTN * HEAD_DIM
D_GMLP = N_GROUPS_GMLP * HEAD_DIM
ROT_DIM = HEAD_DIM // 4
ROPE_THETA = 500000.0
DILATED_PATTERNS = ((128, 1), (512, 4), (2048, 16))
GMLP_CHUNK = 128
N_EXPERTS = 32
TOP_K = 4
SWIGLU_LIMIT = 7.0
SWIGLU_ALPHA = 1.702
N_MOD = 6
EPS = 1e-6
LANES = 128
MASKED = -1e30

VMEM_LIMIT = 56 * 1024 * 1024


def _params(sem, vmem=VMEM_LIMIT):
    return pltpu.CompilerParams(dimension_semantics=sem, vmem_limit_bytes=vmem)


def _mod_kernel(c_ref, w_ref, b_ref, o_ref):
    c = c_ref[...]
    s = c * jax.nn.sigmoid(c)
    o_ref[...] = jnp.dot(s.astype(BF16), w_ref[...].astype(BF16),
                         preferred_element_type=F32) + b_ref[...]


def _mod_base(c, w_mod, b_mod, tn=512):
    B, D = c.shape
    n_out = w_mod.shape[1]
    return pl.pallas_call(
        _mod_kernel,
        grid=(n_out // tn,),
        in_specs=[pl.BlockSpec((B, D), lambda j: (0, 0)),
                  pl.BlockSpec((D, tn), lambda j: (0, j)),
                  pl.BlockSpec((1, tn), lambda j: (0, j))],
        out_specs=pl.BlockSpec((B, tn), lambda j: (0, j)),
        out_shape=jax.ShapeDtypeStruct((B, n_out), F32),
        compiler_params=_params(("parallel",)),
        name="mod_base",
    )(c, w_mod, b_mod.reshape(1, n_out))


def _pack_halves(x):
    c = x.shape[1] // 2
    bits = pltpu.bitcast(x.astype(BF16).astype(F32), jnp.uint32)
    return (bits[:, c:] & jnp.uint32(0xFFFF0000)) | (bits[:, :c] >> 16)


def _unpack_halves(p):
    lo = pltpu.bitcast(p << 16, F32)
    hi = pltpu.bitcast(p & jnp.uint32(0xFFFF0000), F32)
    return lo, hi


def _mod_row(mb_ref, ml_ref, idx):
    return mb_ref[0, idx:idx + 1, :] + ml_ref[idx:idx + 1, :]


def _norm_mod(x, mb_ref, ml_ref, shift_idx, scale_idx):
    y = x * lax.rsqrt(jnp.mean(x * x, axis=-1, keepdims=True) + EPS)
    return y * (1.0 + _mod_row(mb_ref, ml_ref, scale_idx)) + _mod_row(mb_ref, ml_ref, shift_idx)


def _norm_kernel(x_ref, mb_ref, ml_ref, h_ref):
    h_ref[...] = _norm_mod(x_ref[...], mb_ref, ml_ref, 0, 1).astype(h_ref.dtype)


def _mixer_norm(x2, mod_base, mod_l, S, tm=256):
    N, D = x2.shape
    per_b = S // tm
    return pl.pallas_call(
        _norm_kernel,
        grid=(N // tm,),
        in_specs=[pl.BlockSpec((tm, D), lambda i: (i, 0)),
                  pl.BlockSpec((1, N_MOD, D), lambda i: (i // per_b, 0, 0)),
                  pl.BlockSpec((N_MOD, D), lambda i: (0, 0))],
        out_specs=pl.BlockSpec((tm, D), lambda i: (i, 0)),
        out_shape=jax.ShapeDtypeStruct((N, D), BF16),
        compiler_params=_params(("parallel",)),
        name="mixer_norm",
    )(x2, mod_base, mod_l)


def _gelu(x):
    return 0.5 * x * (1.0 + lax.erf(x * (1.0 / math.sqrt(2.0))))


def _proj_kernel(*refs, mode, sub):
    if mode == "gelu_ln":
        h_ref, w_ref, lg_ref, lb_ref, o_ref = refs
    else:
        h_ref, w_ref, o_ref = refs
    bn = o_ref.shape[1]
    for s in range(bn // sub):
        acc = jnp.dot(h_ref[...], w_ref[:, s * sub:(s + 1) * sub], preferred_element_type=F32)
        for hh in range(sub // HEAD_DIM):
            t = acc[:, hh * HEAD_DIM:(hh + 1) * HEAD_DIM]
            sl = slice(s * sub + hh * HEAD_DIM, s * sub + (hh + 1) * HEAD_DIM)
            if mode == "gelu":
                r = _gelu(t)
            elif mode == "gelu_ln":
                g = _gelu(t)
                gc = g - jnp.mean(g, axis=-1, keepdims=True)
                r = gc * lax.rsqrt(jnp.mean(gc * gc, axis=-1, keepdims=True) + EPS)
                r = r * lg_ref[:, sl] + lb_ref[:, sl]
            else:
                r = t
            o_ref[:, sl] = r.astype(o_ref.dtype)


def _proj(h, w, col0, width, mode, extra=(), extra_specs=(), bm=1024, bn=1024, sub=256):
    N, D = h.shape
    j0 = col0 // bn
    return pl.pallas_call(
        functools.partial(_proj_kernel, mode=mode, sub=sub),
        grid=(N // bm, width // bn),
        in_specs=[pl.BlockSpec((bm, D), lambda i, j: (i, 0)),
                  pl.BlockSpec((D, bn), lambda i, j: (0, j0 + j))] + list(extra_specs),
        out_specs=pl.BlockSpec((bm, bn), lambda i, j: (i, j)),
        out_shape=jax.ShapeDtypeStruct((N, width), BF16),
        compiler_params=_params(("parallel", "arbitrary")),
        name="in_proj_" + mode,
    )(h, w, *extra)


MXU_DIM = 256


def _deinterleave_gate_up(w2, g2, u2):
    r = lax.broadcasted_iota(I32, (MXU_DIM, MXU_DIM), 0)
    c = lax.broadcasted_iota(I32, (MXU_DIM, MXU_DIM), 1)
    src = jnp.where(c < LANES, 2 * c, 2 * (c - LANES) + 1)
    perm = jnp.where(r == src, 1.0, 0.0).astype(BF16)
    for ch in range(w2.shape[1] // MXU_DIM):
        w = w2[:, ch * MXU_DIM:(ch + 1) * MXU_DIM].astype(BF16)
        res = jnp.dot(w, perm, preferred_element_type=F32)
        g2[:, ch * LANES:(ch + 1) * LANES] = res[:, :LANES].astype(BF16)
        u2[:, ch * LANES:(ch + 1) * LANES] = res[:, LANES:].astype(BF16)


def _qk_proj_kernel(h_ref, w_ref, gain_ref, cos_ref, s1_ref, s2_ref, *rest):
    if len(rest) == 6:
        side_ref, o_ref, g_ref, u_ref, acc0, acc1 = rest
    else:
        side_ref = None
        o_ref, acc0, acc1 = rest
    t = pl.program_id(0)
    bn = o_ref.shape[1]

    @pl.when(t == 0)
    def _():
        acc1[...] = jnp.zeros_like(acc1)

    def step(acc_mm, acc_ep):
        acc_mm[...] = jnp.dot(h_ref[...], w_ref[...], preferred_element_type=F32)
        if side_ref is not None:
            _deinterleave_gate_up(side_ref.at[0], g_ref, u_ref)
        for hh in range(bn // HEAD_DIM):
            sl = slice(hh * HEAD_DIM, (hh + 1) * HEAD_DIM)
            a = acc_ep[:, sl]
            y = a * lax.rsqrt(jnp.mean(a * a, axis=-1, keepdims=True) + EPS) * gain_ref[0]
            r = (y * cos_ref[...] + pltpu.roll(y, HEAD_DIM - ROT_DIM // 2, 1) * s1_ref[...]
                 + pltpu.roll(y, ROT_DIM // 2, 1) * s2_ref[...])
            o_ref[:, sl] = r.astype(o_ref.dtype)

    @pl.when(t % 2 == 0)
    def _():
        step(acc0, acc1)

    @pl.when(t % 2 == 1)
    def _():
        step(acc1, acc0)


def _qk_proj(h, w, qk_gain, cos_t, s1_t, s2_t, side3, layer, bm=1024, bn=512):
    N, D = h.shape
    width = 2 * D_ATTN
    nj = width // bn
    n_tiles = (N // bm) * nj
    tiles_per_part = D_ATTN // bn
    mm = lambda t: jnp.minimum(t, n_tiles - 1)
    ep = lambda t: jnp.maximum(t - 1, 0)
    tok_tab = pl.BlockSpec((bm, HEAD_DIM), lambda t: (ep(t) // nj, 0))
    _, side_rows, side_cols = side3.shape
    rows = _rider_rows(side_rows, side_cols, n_tiles)
    in_specs = [pl.BlockSpec((bm, D), lambda t: (mm(t) // nj, 0)),
                pl.BlockSpec((D, bn), lambda t: (0, mm(t) % nj)),
                pl.BlockSpec((1, 1, HEAD_DIM), lambda t: ((ep(t) % nj) // tiles_per_part, 0, 0)),
                tok_tab, tok_tab, tok_tab]
    out_specs = [pl.BlockSpec((bm, bn), lambda t: (ep(t) // nj, ep(t) % nj))]
    out_shape = [jax.ShapeDtypeStruct((N, width), BF16)]
    args = [h, w, qk_gain, cos_t, s1_t, s2_t]
    if rows is not None:
        half_spec = pl.BlockSpec((rows, side_cols // 2), lambda t: (mm(t), 0))
        in_specs.append(pl.BlockSpec((1, rows, side_cols), lambda t: (layer, mm(t), 0)))
        out_specs += [half_spec, half_spec]
        out_shape += [jax.ShapeDtypeStruct((side_rows, side_cols // 2), BF16)] * 2
        args.append(side3)
    res = pl.pallas_call(
        _qk_proj_kernel,
        grid=(n_tiles + 1,),
        in_specs=in_specs,
        out_specs=out_specs,
        out_shape=out_shape,
        scratch_shapes=[pltpu.VMEM((bm, bn), F32), pltpu.VMEM((bm, bn), F32)],
        compiler_params=_params(("arbitrary",)),
        name="in_proj_qk",
    )(*args)
    return (res[0], res[1], res[2]) if rows is not None else (res[0], None, None)


def _cast_kernel(x_ref, o_ref):
    o_ref[...] = x_ref[0].astype(o_ref.dtype)


def _to_bf16(w3, layer, tr):
    _, R, C = w3.shape
    return pl.pallas_call(
        _cast_kernel,
        grid=(R // tr,),
        in_specs=[pl.BlockSpec((1, tr, C), lambda i: (layer, i, 0))],
        out_specs=pl.BlockSpec((tr, C), lambda i: (i, 0)),
        out_shape=jax.ShapeDtypeStruct((R, C), BF16),
        compiler_params=_params(("parallel",)),
        name="to_bf16",
    )(w3)


def _in_proj(h, w_in, qk_gain, cos_t, s1_t, s2_t, ln_g, ln_b, gate_up3, layer, bm=1024, bn=1024):
    qk, wg, wu = _qk_proj(h, w_in, qk_gain, cos_t, s1_t, s2_t, gate_up3, layer)
    v = _proj(h, w_in, 2 * D_ATTN, D_ATTN, "none", bm=bm, bn=bn)
    u = _proj(h, w_in, 3 * D_ATTN, D_GMLP, "gelu", bm=bm, bn=bn)
    row = pl.BlockSpec((1, bn), lambda i, j: (0, j))
    gv = _proj(h, w_in, 3 * D_ATTN + D_GMLP, D_GMLP, "gelu_ln", (ln_g, ln_b), (row, row), bm, bn)
    return qk, v, u, gv, wg, wu


def _attn_kernel(q_ref, k_ref, v_ref, bias_ref, *rest, tq):
    n_cast = (len(rest) - 1) // 2
    o_ref = rest[n_cast]
    for src, dst in zip(rest[:n_cast], rest[n_cast + 1:]):
        dst[...] = src[0].astype(dst.dtype)
    S = q_ref.shape[1]
    n_q = S // tq
    for i in range(n_q):
        kend = (i + 1) * tq
        off = (n_q - 1 - i) * tq
        for hh in range(q_ref.shape[2] // HEAD_DIM):
            cols = slice(hh * HEAD_DIM, (hh + 1) * HEAD_DIM)
            q = q_ref[0, i * tq:(i + 1) * tq, cols]
            s = lax.dot_general(q, k_ref[0, :kend, cols], (((1,), (1,)), ((), ())),
                                preferred_element_type=F32)
            s = s + bias_ref[:, off:off + kend]
            m = jnp.max(s, axis=-1, keepdims=True)
            p = jnp.exp(s - m)
            l = jnp.sum(p, axis=-1, keepdims=True)
            o = jnp.dot(p.astype(BF16), v_ref[0, :kend, cols], preferred_element_type=F32)
            o_ref[0, i * tq:(i + 1) * tq, cols] = (o / l).astype(o_ref.dtype)


def _attn_bias_table(S, tq):
    r = jnp.arange(tq, dtype=I32)[:, None]
    u = jnp.arange(S, dtype=I32)[None, :]
    d = r + (S - tq) - u
    mult = jnp.zeros((tq, S), F32)
    for window, dil in DILATED_PATTERNS:
        mult = mult + ((d >= 0) & (d % dil == 0) & (d <= window)).astype(F32)
    return jnp.where(mult > 0, jnp.log(jnp.maximum(mult, 1.0)), MASKED)


def _attention(qk3, v3, casts=(), tq=256, heads_per_step=2):
    B, S, _ = v3.shape
    bias = _attn_bias_table(S, tq)
    H = N_HEADS_ATTN // heads_per_step
    width = heads_per_step * HEAD_DIM
    head_blk = pl.BlockSpec((1, S, width), lambda b, h: (b, 0, h))
    in_specs = [head_blk, pl.BlockSpec((1, S, width), lambda b, h: (b, 0, H + h)), head_blk,
                pl.BlockSpec((tq, S), lambda b, h: (0, 0))]
    out_specs = [head_blk]
    out_shape = [jax.ShapeDtypeStruct((B, S, D_ATTN), BF16)]
    args = [qk3, qk3, v3, bias]
    riding = []
    for arr3, layer in casts:
        _, total_rows, cols = arr3.shape
        rows = _rider_rows(total_rows, cols, B * H)
        riding.append(rows is not None)
        if rows is not None:
            in_specs.append(pl.BlockSpec((1, rows, cols), lambda b, h, layer=layer: (layer, b * H + h, 0)))
            out_specs.append(pl.BlockSpec((rows, cols), lambda b, h: (b * H + h, 0)))
            out_shape.append(jax.ShapeDtypeStruct((total_rows, cols), BF16))
            args.append(arr3)
    res = pl.pallas_call(
        functools.partial(_attn_kernel, tq=tq),
        grid=(B, H),
        in_specs=in_specs,
        out_specs=out_specs,
        out_shape=out_shape,
        compiler_params=_params(("parallel", "parallel")),
        name="dilated_attn",
    )(*args)
    outs = iter(res[1:])
    return res[0], [next(outs) if r else None for r in riding]


def _gmlp_kernel(a_ref, u_ref, gv_ref, ws_ref, bst_ref, on_ref, o_ref, g_scr):
    tg = a_ref.shape[0]
    row = lax.broadcasted_iota(I32, (GMLP_CHUNK, GMLP_CHUNK), 0)
    col = lax.broadcasted_iota(I32, (GMLP_CHUNK, GMLP_CHUNK), 1)
    causal = row >= col
    for g in range(N_GROUPS_GMLP):
        w = jnp.where(causal, ws_ref[g], 0.0).astype(BF16)
        bias = bst_ref[:, g:g + 1]
        sl = slice(g * HEAD_DIM, (g + 1) * HEAD_DIM)
        for c in range(tg // GMLP_CHUNK):
            rows = slice(c * GMLP_CHUNK, (c + 1) * GMLP_CHUNK)
            mixed = jnp.dot(w, gv_ref[rows, sl], preferred_element_type=F32) + bias
            g_scr[rows, sl] = u_ref[rows, sl].astype(F32) * mixed
    gm = g_scr[...]
    gm = gm * lax.rsqrt(jnp.mean(gm * gm, axis=-1, keepdims=True) + EPS)
    o_ref[:, D_ATTN:] = (gm * on_ref[:, D_ATTN:]).astype(o_ref.dtype)
    a = a_ref[...].astype(F32)
    a = a * lax.rsqrt(jnp.mean(a * a, axis=-1, keepdims=True) + EPS)
    o_ref[:, :D_ATTN] = (a * on_ref[:, :D_ATTN]).astype(o_ref.dtype)


def _gmlp_and_norms(attn2, u, gv, w_s, b_s_t, out_norm, tg=512):
    N = attn2.shape[0]
    D_mix = D_ATTN + D_GMLP
    return pl.pallas_call(
        _gmlp_kernel,
        grid=(N // tg,),
        in_specs=[pl.BlockSpec((tg, D_ATTN), lambda i: (i, 0)),
                  pl.BlockSpec((tg, D_GMLP), lambda i: (i, 0)),
                  pl.BlockSpec((tg, D_GMLP), lambda i: (i, 0)),
                  pl.BlockSpec((N_GROUPS_GMLP, GMLP_CHUNK, GMLP_CHUNK), lambda i: (0, 0, 0)),
                  pl.BlockSpec((GMLP_CHUNK, N_GROUPS_GMLP), lambda i: (0, 0)),
                  pl.BlockSpec((1, D_mix), lambda i: (0, 0))],
        out_specs=pl.BlockSpec((tg, D_mix), lambda i: (i, 0)),
        out_shape=jax.ShapeDtypeStruct((N, D_mix), BF16),
        scratch_shapes=[pltpu.VMEM((tg, D_GMLP), F32)],
        compiler_params=_params(("parallel",)),
        name="gmlp_norms",
    )(attn2, u, gv, w_s, b_s_t, out_norm)


RIDER_MAX_BYTES = 8 * 1024 * 1024


def _rider_rows(total_rows, cols, steps):
    if total_rows % steps:
        return None
    rows = total_rows // steps
    if rows % 16 or rows * cols * 4 > RIDER_MAX_BYTES:
        return None
    return rows


def _outproj_kernel(m_ref, w_ref, x_ref, mb_ref, ml_ref, *rest):
    if len(rest) == 3:
        side_ref, o_ref, side_out = rest
        side_out[...] = side_ref[0].astype(side_out.dtype)
    else:
        o_ref, = rest
    acc = jnp.dot(m_ref[...], w_ref[...], preferred_element_type=F32)
    o_ref[...] = x_ref[...] + _mod_row(mb_ref, ml_ref, 2) * acc


def _out_proj(mix, w_out, x2, mod_base, mod_l, S, side3, layer, bm=2048, bn=256):
    N, D_mix = mix.shape
    D = w_out.shape[1]
    per_b = S // bm
    nj = D // bn
    rows = None
    if side3 is not None:
        _, side_rows, side_cols = side3.shape
        rows = _rider_rows(side_rows, side_cols, (N // bm) * nj)
    in_specs = [pl.BlockSpec((bm, D_mix), lambda i, j: (i, 0)),
                pl.BlockSpec((D_mix, bn), lambda i, j: (0, j)),
                pl.BlockSpec((bm, bn), lambda i, j: (i, j)),
                pl.BlockSpec((1, N_MOD, bn), lambda i, j: (i // per_b, 0, j)),
                pl.BlockSpec((N_MOD, bn), lambda i, j: (0, j))]
    out_specs = [pl.BlockSpec((bm, bn), lambda i, j: (i, j))]
    out_shape = [jax.ShapeDtypeStruct((N, D), F32)]
    args = [mix, w_out, x2, mod_base, mod_l]
    if rows is not None:
        in_specs.append(pl.BlockSpec((1, rows, side_cols), lambda i, j: (layer, i * nj + j, 0)))
        out_specs.append(pl.BlockSpec((rows, side_cols), lambda i, j: (i * nj + j, 0)))
        out_shape.append(jax.ShapeDtypeStruct((side_rows, side_cols), BF16))
        args.append(side3)
    res = pl.pallas_call(
        _outproj_kernel,
        grid=(N // bm, nj),
        in_specs=in_specs,
        out_specs=out_specs,
        out_shape=out_shape,
        compiler_params=_params(("parallel", "arbitrary")),
        name="out_proj",
    )(*args)
    return (res[0], res[1]) if rows is not None else (res[0], None)


def _router_kernel(x_ref, mb_ref, ml_ref, wr_ref, br_ref,
                   h_ref, idx_ref, gate_ref, rank_ref, cnt_ref, cnt_scr):
    i = pl.program_id(0)
    tm = x_ref.shape[0]

    @pl.when(i == 0)
    def _():
        cnt_scr[...] = jnp.zeros_like(cnt_scr)

    h = _norm_mod(x_ref[...], mb_ref, ml_ref, 3, 4)
    h_ref[...] = _pack_halves(h)
    logits = jnp.dot(h.astype(BF16), wr_ref[...].astype(BF16),
                     preferred_element_type=F32) + br_ref[...]
    lane = lax.broadcasted_iota(I32, (tm, LANES), 1)
    lane_f = lane.astype(F32)
    vals = jnp.where(lane < N_EXPERTS, logits, -jnp.inf)
    tops, sels = [], []
    idx_out = jnp.zeros((tm, LANES), F32)
    for k in range(TOP_K):
        m = jnp.max(vals, axis=-1, keepdims=True)
        first = jnp.min(jnp.where(vals == m, lane_f, float(LANES)), axis=-1, keepdims=True)
        sel = lane_f == first
        tops.append(m)
        sels.append(sel)
        idx_out = jnp.where(lane == k, first, idx_out)
        vals = jnp.where(sel, -jnp.inf, vals)
    exps = [jnp.exp(t - tops[0]) for t in tops]
    denom = exps[0] + exps[1] + exps[2] + exps[3]
    gate_out = jnp.zeros((tm, LANES), F32)
    for k in range(TOP_K):
        gate_out = jnp.where(lane == k, exps[k] / denom, gate_out)

    chosen = jnp.zeros((tm, LANES), F32)
    for sel in sels:
        chosen = chosen + sel.astype(F32)
    r = lax.broadcasted_iota(I32, (tm, tm), 0)
    c = lax.broadcasted_iota(I32, (tm, tm), 1)
    earlier = jnp.where(r > c, 1.0, 0.0).astype(BF16)
    before = jnp.dot(earlier, chosen.astype(BF16), preferred_element_type=F32) + cnt_scr[...]
    rank_out = jnp.zeros((tm, LANES), F32)
    for k in range(TOP_K):
        rk = jnp.sum(jnp.where(sels[k], before, 0.0), axis=-1, keepdims=True)
        rank_out = jnp.where(lane == k, rk, rank_out)
    cnt_scr[...] = cnt_scr[...] + jnp.sum(chosen, axis=0, keepdims=True)

    idx_ref[...] = idx_out.astype(I32)
    gate_ref[...] = gate_out
    rank_ref[...] = rank_out.astype(I32)
    cnt_ref[...] = cnt_scr[...].astype(I32)


def _ffn_norm_router(x2, mod_base, mod_l, w_router_p, b_router_p, S, tm=256):
    N, D = x2.shape
    per_b = S // tm
    tok_lane = pl.BlockSpec((tm, LANES), lambda i: (i, 0))
    return pl.pallas_call(
        _router_kernel,
        grid=(N // tm,),
        in_specs=[pl.BlockSpec((tm, D), lambda i: (i, 0)),
                  pl.BlockSpec((1, N_MOD, D), lambda i: (i // per_b, 0, 0)),
                  pl.BlockSpec((N_MOD, D), lambda i: (0, 0)),
                  pl.BlockSpec((D, LANES), lambda i: (0, 0)),
                  pl.BlockSpec((1, LANES), lambda i: (0, 0))],
        out_specs=[pl.BlockSpec((tm, D // 2), lambda i: (i, 0)), tok_lane, tok_lane, tok_lane,
                   pl.BlockSpec((1, LANES), lambda i: (0, 0))],
        out_shape=[jax.ShapeDtypeStruct((N, D // 2), jnp.uint32),
                   jax.ShapeDtypeStruct((N, LANES), I32),
                   jax.ShapeDtypeStruct((N, LANES), F32),
                   jax.ShapeDtypeStruct((N, LANES), I32),
                   jax.ShapeDtypeStruct((1, LANES), I32)],
        scratch_shapes=[pltpu.VMEM((1, LANES), F32)],
        compiler_params=_params(("arbitrary",)),
        name="ffn_norm_router",
    )(x2, mod_base, mod_l, w_router_p, b_router_p)


def _split_gate_up_kernel(w_ref, g_ref, u_ref):
    _deinterleave_gate_up(w_ref.at[0, 0], g_ref.at[0], u_ref.at[0])


def _split_gate_up(w_gate_up, layer, tr=1024):
    _, E, D, two_ff = w_gate_up.shape
    d_ff = two_ff // 2
    out = jax.ShapeDtypeStruct((E, D, d_ff), BF16)
    return pl.pallas_call(
        _split_gate_up_kernel,
        grid=(E, D // tr),
        in_specs=[pl.BlockSpec((1, 1, tr, two_ff), lambda e, i: (layer, e, i, 0))],
        out_specs=[pl.BlockSpec((1, tr, d_ff), lambda e, i: (e, i, 0)),
                   pl.BlockSpec((1, tr, d_ff), lambda e, i: (e, i, 0))],
        out_shape=[out, out],
        compiler_params=_params(("parallel", "parallel")),
        name="split_gate_up",
    )(w_gate_up)


def _dispatch_kernel(dest_ref, pad_start_ref, pad_len_ref, nused_ref, h_ref, xs_hbm,
                     zero_buf, sem, pad_sem, blk_sem, *, td, tm):
    i = pl.program_id(0)
    n = pl.num_programs(0)
    n_blocks = xs_hbm.shape[0] // tm

    def body(r, carry):
        base = (i * td + r) * TOP_K
        for k in range(TOP_K):
            pltpu.make_async_copy(h_ref.at[pl.ds(r, 1), :], xs_hbm.at[pl.ds(dest_ref[base + k], 1), :],
                                  sem).start(priority=k % 2)
        return carry
    lax.fori_loop(0, td, body, 0, unroll=2)

    for k in range(TOP_K):
        pltpu.make_async_copy(h_ref, xs_hbm.at[pl.ds(0, td), :], sem).wait()

    @pl.when(i == n - 1)
    def _():
        zero_buf[...] = jnp.zeros_like(zero_buf)

        def per_expert(e, total):
            start = pad_start_ref[e]
            count = pad_len_ref[e]

            def fill(r, carry):
                pltpu.make_async_copy(zero_buf.at[pl.ds(0, 1), :], xs_hbm.at[pl.ds(start + r, 1), :],
                                      pad_sem).start()
                return carry
            lax.fori_loop(0, count, fill, 0)
            return total + count
        total = lax.fori_loop(0, N_EXPERTS, per_expert, 0)

        def block_copy(blk):
            return pltpu.make_async_copy(zero_buf, xs_hbm.at[pl.ds(pl.multiple_of(blk * tm, tm), tm), :],
                                         blk_sem)

        def fill_block(blk, carry):
            block_copy(blk).start()
            return carry
        lax.fori_loop(nused_ref[0], n_blocks, fill_block, 0)

        def retire(r, carry):
            pltpu.make_async_copy(zero_buf.at[pl.ds(0, 1), :], xs_hbm.at[pl.ds(0, 1), :], pad_sem).wait()
            return carry
        lax.fori_loop(0, total, retire, 0)

        def retire_block(blk, carry):
            block_copy(blk).wait()
            return carry
        lax.fori_loop(nused_ref[0], n_blocks, retire_block, 0)


def _dispatch(dest_flat, pad_start, pad_len, nused, h2, n_slots, tm, td=1024):
    N, half = h2.shape
    grid_spec = pltpu.PrefetchScalarGridSpec(
        num_scalar_prefetch=4,
        grid=(N // td,),
        in_specs=[pl.BlockSpec((td, half), lambda i, d, ps, pn, nu: (i, 0))],
        out_specs=pl.BlockSpec(memory_space=pl.ANY),
        scratch_shapes=[pltpu.VMEM((tm, half), jnp.uint32), pltpu.SemaphoreType.DMA(()),
                        pltpu.SemaphoreType.DMA(()), pltpu.SemaphoreType.DMA(())],
    )
    return pl.pallas_call(
        functools.partial(_dispatch_kernel, td=td, tm=tm),
        grid_spec=grid_spec,
        out_shape=jax.ShapeDtypeStruct((n_slots, half), jnp.uint32),
        compiler_params=_params(("arbitrary",)),
        name="moe_dispatch",
    )(dest_flat, pad_start, pad_len, nused, h2)


def _expert_kernel(be_ref, nused_ref, x_ref, wg_ref, wu_ref, wd_ref, bg_ref, bu_ref, bd_ref, y_ref):
    b = pl.program_id(0)

    @pl.when(b < nused_ref[0])
    def _():
        lo, hi = _unpack_halves(x_ref[...])
        x = jnp.concatenate([lo.astype(BF16), hi.astype(BF16)], axis=1)
        gate = jnp.dot(x, wg_ref[0], preferred_element_type=F32) + bg_ref[0]
        up = jnp.dot(x, wu_ref[0], preferred_element_type=F32) + bu_ref[0]
        gate = jnp.minimum(gate, SWIGLU_LIMIT)
        up = jnp.clip(up, -SWIGLU_LIMIT, SWIGLU_LIMIT)
        act = (up + 1.0) * gate * jax.nn.sigmoid(SWIGLU_ALPHA * gate)
        y = jnp.dot(act.astype(BF16), wd_ref[0], preferred_element_type=F32) + bd_ref[0]
        y_ref[...] = _pack_halves(y)

    @pl.when(b >= nused_ref[0])
    def _():
        y_ref[...] = jnp.zeros_like(y_ref)


def _experts(xs, block_e, nused, wg, wu, wd, bg, bu, bd, tm):
    half = xs.shape[1]
    D = 2 * half
    n_blocks = block_e.shape[0]
    d_ff = wg.shape[2]
    e_map = lambda b, be, nu: (be[b], 0, 0)
    x_map = lambda b, be, nu: (jnp.minimum(b, nu[0] - 1), 0)
    grid_spec = pltpu.PrefetchScalarGridSpec(
        num_scalar_prefetch=2,
        grid=(n_blocks,),
        in_specs=[pl.BlockSpec((tm, half), x_map),
                  pl.BlockSpec((1, D, d_ff), e_map),
                  pl.BlockSpec((1, D, d_ff), e_map),
                  pl.BlockSpec((1, d_ff, D), e_map),
                  pl.BlockSpec((1, 1, d_ff), e_map),
                  pl.BlockSpec((1, 1, d_ff), e_map),
                  pl.BlockSpec((1, 1, D), e_map)],
        out_specs=pl.BlockSpec((tm, half), lambda b, be, nu: (b, 0)),
    )
    return pl.pallas_call(
        _expert_kernel,
        grid_spec=grid_spec,
        out_shape=jax.ShapeDtypeStruct((n_blocks * tm, half), jnp.uint32),
        compiler_params=_params(("arbitrary",)),
        name="experts",
    )(block_e, nused, xs, wg, wu, wd, bg, bu, bd)


SUBLANES = 8


def _combine_kernel(dest_ref, y_hbm, x_ref, gate_ref, mb_ref, ml_ref, o_ref, buf0, buf1, sem, *, tc):
    i = pl.program_id(0)
    n = pl.num_programs(0)
    bufs = (buf0, buf1)
    half = buf0.shape[2]

    def row_copy(d, slot, k, r):
        return pltpu.make_async_copy(y_hbm.at[pl.ds(d, 1), :],
                                     bufs[slot].at[k, pl.ds(r, 1), :], sem.at[slot])

    def issue_rows(step, slot, r0, count):
        for rr in range(count):
            base = (step * tc + r0 + rr) * TOP_K
            for k in range(TOP_K):
                row_copy(dest_ref[base + k], slot, k, r0 + rr).start(priority=k % 2)

    def wait_step(slot):
        for k in range(TOP_K):
            pltpu.make_async_copy(y_hbm.at[pl.ds(0, tc), :], bufs[slot].at[k], sem.at[slot]).wait()

    @pl.when(i == 0)
    def _():
        def body(c, carry):
            issue_rows(0, 0, c * SUBLANES, SUBLANES)
            return carry
        lax.fori_loop(0, tc // SUBLANES, body, 0)

    nxt = jnp.minimum(i + 1, n - 1)
    gate_f = _mod_row(mb_ref, ml_ref, 5)

    def step(slot):
        wait_step(slot)

        def body(c, carry):
            r0 = pl.multiple_of(c * SUBLANES, SUBLANES)
            issue_rows(nxt, 1 - slot, r0, SUBLANES)
            rows = pl.ds(r0, SUBLANES)
            acc_lo = jnp.zeros((SUBLANES, half), F32)
            acc_hi = jnp.zeros((SUBLANES, half), F32)
            for k in range(TOP_K):
                lo, hi = _unpack_halves(bufs[slot][k, rows, :])
                g = gate_ref[rows, k:k + 1]
                acc_lo = acc_lo + g * lo
                acc_hi = acc_hi + g * hi
            o_ref[rows, :half] = x_ref[rows, :half] + gate_f[:, :half] * acc_lo
            o_ref[rows, half:] = x_ref[rows, half:] + gate_f[:, half:] * acc_hi
            return carry
        lax.fori_loop(0, tc // SUBLANES, body, 0)

        @pl.when(i == n - 1)
        def _():
            wait_step(1 - slot)

    for slot in range(2):
        @pl.when(i % 2 == slot)
        def _(slot=slot):
            step(slot)


def _combine(dest_flat, y, x2, gates, mod_base, mod_l, S, tc=256):
    N, D = x2.shape
    per_b = S // tc
    grid_spec = pltpu.PrefetchScalarGridSpec(
        num_scalar_prefetch=1,
        grid=(N // tc,),
        in_specs=[pl.BlockSpec(memory_space=pl.ANY),
                  pl.BlockSpec((tc, D), lambda i, d: (i, 0)),
                  pl.BlockSpec((tc, LANES), lambda i, d: (i, 0)),
                  pl.BlockSpec((1, N_MOD, D), lambda i, d: (i // per_b, 0, 0)),
                  pl.BlockSpec((N_MOD, D), lambda i, d: (0, 0))],
        out_specs=pl.BlockSpec((tc, D), lambda i, d: (i, 0)),
        scratch_shapes=[pltpu.VMEM((TOP_K, tc, D // 2), jnp.uint32),
                        pltpu.VMEM((TOP_K, tc, D // 2), jnp.uint32), pltpu.SemaphoreType.DMA((2,))],
    )
    return pl.pallas_call(
        functools.partial(_combine_kernel, tc=tc),
        grid_spec=grid_spec,
        out_shape=jax.ShapeDtypeStruct((N, D), F32),
        compiler_params=_params(("arbitrary",)),
        name="moe_combine",
    )(dest_flat, y, x2, gates, mod_base, mod_l)


EXPERT_ROWS = 256


def _rope_tables(positions):
    half = ROT_DIM // 2
    inv_freq = ROPE_THETA ** (-jnp.arange(half, dtype=F32) * 2.0 / ROT_DIM)
    ang = positions.astype(F32).reshape(-1, 1) * inv_freq
    cos, sin = jnp.cos(ang), jnp.sin(ang)
    n = ang.shape[0]
    ones = jnp.ones((n, HEAD_DIM - ROT_DIM), F32)
    zeros = jnp.zeros((n, HEAD_DIM - half), F32)
    cos_t = jnp.concatenate([cos, cos, ones], axis=1)
    s1_t = jnp.concatenate([-sin, zeros], axis=1)
    s2_t = jnp.concatenate([jnp.zeros((n, half), F32), sin,
                            jnp.zeros((n, HEAD_DIM - ROT_DIM), F32)], axis=1)
    return cos_t, s1_t, s2_t


def _routing_tables(top_idx, rank, counts, tm, n_blocks):
    N = top_idx.shape[0]
    nblk = (counts + tm - 1) // tm
    blk_end = jnp.cumsum(nblk)
    blk_start = blk_end - nblk
    dest = blk_start[top_idx] * tm + rank
    owner = jnp.sum((blk_end[None, :] <= jnp.arange(n_blocks, dtype=I32)[:, None]).astype(I32), axis=1)
    block_e = jnp.minimum(owner, N_EXPERTS - 1).astype(I32)
    nused = blk_end[-1:].astype(I32)
    pad_start = (blk_start * tm + counts).astype(I32)
    pad_len = (nblk * tm - counts).astype(I32)
    return dest.reshape(-1).astype(I32), block_e, nused, pad_start, pad_len


def kernel(x, c, positions, w_mod, b_mod, mod_layer, w_in, q_norm, k_norm, w_s, b_s, v_ln_g, v_ln_b,
           out_norm, w_out, w_router, b_router, w_gate_up, b_gate_up, w_down, b_down):
    B, S, D = x.shape
    N = B * S
    depth = w_in.shape[0]
    d_ff = w_down.shape[2]
    tm_e = EXPERT_ROWS
    n_blocks = N * TOP_K // tm_e + N_EXPERTS

    mod_base = _mod_base(c, w_mod, b_mod).reshape(B, N_MOD, D)
    cos_t, s1_t, s2_t = _rope_tables(positions)
    x2 = x.reshape(N, D)

    w_in_bf = None
    for l in range(depth):
        mod_l = mod_layer[l]
        h = _mixer_norm(x2, mod_base, mod_l, S)
        qk_gain = jnp.stack([q_norm[l] * HEAD_DIM ** -0.5, k_norm[l]]).reshape(2, 1, HEAD_DIM)
        if w_in_bf is None:
            w_in_bf = _to_bf16(w_in, l, 256)
        qk, v, u, gv, wg, wu = _in_proj(h, w_in_bf, qk_gain, cos_t, s1_t, s2_t,
                                        v_ln_g[l].reshape(1, D_GMLP), v_ln_b[l].reshape(1, D_GMLP),
                                        w_gate_up.reshape(depth, N_EXPERTS * D, 2 * d_ff), l)
        w_down3 = w_down.reshape(depth, N_EXPERTS * d_ff, D)
        casts = [(w_out, l), (w_down3, l)] + ([(w_in, l + 1)] if l + 1 < depth else [])
        attn, cast_res = _attention(qk.reshape(B, S, -1), v.reshape(B, S, -1), casts)
        w_out_bf = cast_res[0] if cast_res[0] is not None else _to_bf16(w_out, l, 512)
        wd_att = cast_res[1]
        w_in_bf = cast_res[2] if l + 1 < depth else None
        mix = _gmlp_and_norms(attn.reshape(N, D_ATTN), u, gv, w_s[l], jnp.transpose(b_s[l]),
                              out_norm[l].reshape(1, -1))
        x2, wd = _out_proj(mix, w_out_bf, x2, mod_base, mod_l, S, None if wd_att is not None else w_down3, l)
        if wd_att is not None:
            wd = wd_att

        w_router_p = jnp.pad(w_router[l], ((0, 0), (0, LANES - N_EXPERTS)))
        b_router_p = jnp.pad(b_router[l], (0, LANES - N_EXPERTS)).reshape(1, LANES)
        h2, top_idx, gates, rank, counts = _ffn_norm_router(x2, mod_base, mod_l, w_router_p,
                                                            b_router_p, S)
        dest, block_e, nused, pad_start, pad_len = _routing_tables(
            top_idx[:, :TOP_K], rank[:, :TOP_K], counts[0, :N_EXPERTS], tm_e, n_blocks)
        xs = _dispatch(dest, pad_start, pad_len, nused, h2, n_blocks * tm_e, tm_e)
        if wg is None:
            wg, wu = _split_gate_up(w_gate_up, l)
        if wd is None:
            wd = _to_bf16(w_down3, l, 512)
        wg = wg.reshape(N_EXPERTS, D, d_ff)
        wu = wu.reshape(N_EXPERTS, D, d_ff)
        wd = wd.reshape(N_EXPERTS, d_ff, D)
        bg = b_gate_up[l][:, 0::2].reshape(N_EXPERTS, 1, d_ff)
        bu = b_gate_up[l][:, 1::2].reshape(N_EXPERTS, 1, d_ff)
        y = _experts(xs, block_e, nused, wg, wu, wd, bg, bu,
                     b_down[l].reshape(N_EXPERTS, 1, D), tm_e)
        x2 = _combine(dest, y, x2, gates, mod_base, mod_l, S)
    return x2.reshape(B, S, D)
```
